```python
import jax, jax.numpy as jnp
from jax import lax
import numpy as np

D_MODEL = 1024
BATCH = 16
SEQ = 4096
DEPTH = 4

N_HEADS = 16
HEAD_DIM = D_MODEL // N_HEADS
D_FF = -(-8 * D_MODEL // (3 * 256)) * 256
N_A_LAYERS = DEPTH // 2
N_B_LAYERS = DEPTH - N_A_LAYERS
BLOCK = 128
ROPE_THETA = 10000.0
NORM_EPS = 1e-6
DILATED_BRANCHES = ((128, 1), (512, 4), (2048, 16))

kernel_name = "yoco_stickbreak_dilated_hybrid"


def _rmsnorm(h, g):
    hf = h.astype(jnp.float32)
    y = hf * lax.rsqrt(jnp.mean(hf * hf, axis=-1, keepdims=True) + NORM_EPS)
    return (y * g.astype(jnp.float32)).astype(h.dtype)


def _rope(a):
    s, d = a.shape[1], a.shape[3]
    inv_freq = ROPE_THETA ** (-jnp.arange(0, d, 2, dtype=jnp.float32) / d)
    ang = jnp.arange(s, dtype=jnp.float32)[:, None] * inv_freq[None, :]
    cos = jnp.cos(ang)[:, None, :]
    sin = jnp.sin(ang)[:, None, :]
    af = a.astype(jnp.float32)
    a1, a2 = af[..., : d // 2], af[..., d // 2:]
    return jnp.concatenate([a1 * cos - a2 * sin, a2 * cos + a1 * sin], axis=-1).astype(a.dtype)


def _swiglu(h, w_gate, w_up, w_down):
    return (jax.nn.silu(h @ w_gate) * (h @ w_up)) @ w_down


def _stick_breaking(q, k, v):
    b, s, h, d = q.shape
    nq = s // BLOCK
    scale = d ** -0.5
    kh = k.transpose(0, 2, 1, 3)
    vh = v.transpose(0, 2, 1, 3)
    qblk = q.reshape(b, nq, BLOCK, h, d).transpose(1, 0, 3, 2, 4)
    key_pos = jnp.arange(s)

    def one_block(args):
        qb, bi = args
        qpos = bi * BLOCK + jnp.arange(BLOCK)
        strict = key_pos[None, :] < qpos[:, None]
        z = jnp.einsum('bhqd,bhkd->bhqk', qb, kh).astype(jnp.float32) * scale
        log_stay = jnp.where(strict, jax.nn.log_sigmoid(-z), 0.0)
        log_after = lax.cumsum(log_stay, axis=3, reverse=True) - log_stay
        w = jnp.where(strict, jnp.exp(jax.nn.log_sigmoid(z) + log_after), 0.0)
        return jnp.einsum('bhqk,bhkd->bhqd', w, vh.astype(jnp.float32)).astype(q.dtype)

    out = lax.map(one_block, (qblk, jnp.arange(nq)))
    return out.transpose(1, 0, 3, 2, 4).reshape(b, s, h, d)


def _to_blocks(a, r):
    b, s, h, d = a.shape
    n = s // r
    nb = -(-n // BLOCK)
    a = a.reshape(b, n, r, h, d).transpose(0, 2, 3, 1, 4)
    a = jnp.pad(a, ((0, 0), (0, 0), (0, 0), (0, nb * BLOCK - n), (0, 0)))
    return a.reshape(b, r, h, nb, BLOCK, d)


def _from_blocks(a, s):
    b, r, h, nb, blk, e = a.shape
    n = s // r
    a = a.reshape(b, r, h, nb * blk, e)[:, :, :, :n]
    return a.transpose(0, 3, 1, 2, 4).reshape(b, s, h, e)


def _dilated_branch(qb, kb, vb, steps):
    nb = qb.shape[3]
    scale = qb.shape[-1] ** -0.5
    k_prev = jnp.concatenate([jnp.zeros_like(kb[:, :, :, :1]), kb[:, :, :, :-1]], axis=3)
    v_prev = jnp.concatenate([jnp.zeros_like(vb[:, :, :, :1]), vb[:, :, :, :-1]], axis=3)
    k_band = jnp.concatenate([k_prev, kb], axis=4)
    v_band = jnp.concatenate([v_prev, vb], axis=4)
    z = jnp.einsum('brhnqd,brhnkd->brhnqk', qb, k_band).astype(jnp.float32) * scale
    qi = jnp.arange(BLOCK)[:, None]
    kj = jnp.arange(2 * BLOCK)[None, :]
    dist = BLOCK + qi - kj
    key_idx = (jnp.arange(nb)[:, None, None] - 1) * BLOCK + kj[None]
    valid = (dist >= 0)[None] & (dist <= steps)[None] & (key_idx >= 0)
    z = jnp.where(valid, z, -jnp.inf)
    m = jnp.max(z, axis=-1, keepdims=True)
    p = jnp.exp(z - m)
    den = jnp.sum(p, axis=-1, keepdims=True)
    o = jnp.einsum('brhnqk,brhnkd->brhnqd', p, v_band.astype(jnp.float32)) / den
    lse = m + jnp.log(den)
    return o, lse


def _dilated_attention(q, kv_blocks):
    s = q.shape[1]
    outs, lses = [], []
    for (window, r), (kb, vb) in zip(DILATED_BRANCHES, kv_blocks):
        o, lse = _dilated_branch(_to_blocks(q, r), kb, vb, window // r)
        outs.append(_from_blocks(o, s))
        lses.append(_from_blocks(lse, s))
    wts = jax.nn.softmax(jnp.stack(lses, axis=0), axis=0)
    return jnp.sum(wts * jnp.stack(outs, axis=0), axis=0).astype(q.dtype)


def setup_inputs(seed: int = 0) -> dict:
    key = jax.random.key(seed)
    ks = jax.random.split(key, 14)
    d, f = D_MODEL, D_FF
    nrm = lambda k, shape, fan: jax.random.normal(k, shape, jnp.float32) * fan ** -0.5
    gain = lambda k, shape: 1.0 + 0.02 * jax.random.normal(k, shape, jnp.float32)
    return {
        "x": jax.random.normal(ks[0], (BATCH, SEQ, d), jnp.float32),
        "norm_mix": gain(ks[1], (DEPTH, d)),
        "w_qkv_a": nrm(ks[2], (N_A_LAYERS, d, 3 * d), d),
        "w_o_a": nrm(ks[3], (N_A_LAYERS, d, d), d),
        "norm_kv": gain(ks[4], (d,)),
        "w_kv": nrm(ks[5], (d, 2 * d), d),
        "w_q_b": nrm(ks[6], (N_B_LAYERS, d, d), d),
        "w_o_b": nrm(ks[7], (N_B_LAYERS, d, d), d),
        "norm_ffn": gain(ks[8], (DEPTH, d)),
        "w_gate": nrm(ks[9], (DEPTH, d, f), d),
        "w_up": nrm(ks[10], (DEPTH, d, f), d),
        "w_down": nrm(ks[11], (DEPTH, f, d), f),
        "norm_final": gain(ks[12], (d,)),
    }


def reference(x, norm_mix, w_qkv_a, w_o_a, norm_kv, w_kv, w_q_b, w_o_b,
              norm_ffn, w_gate, w_up, w_down, norm_final):
    b, s, d = x.shape
    h = x
    kv_blocks = None
    for layer in range(DEPTH):
        hn = _rmsnorm(h, norm_mix[layer])
        if layer < N_A_LAYERS:
            q, k, v = jnp.split(hn @ w_qkv_a[layer], 3, axis=-1)
            q = q.reshape(b, s, N_HEADS, HEAD_DIM)
            k = k.reshape(b, s, N_HEADS, HEAD_DIM)
            v = v.reshape(b, s, N_HEADS, HEAD_DIM)
            o = _stick_breaking(q, k, v).reshape(b, s, d)
            h = h + o @ w_o_a[layer]
        else:
            j = layer - N_A_LAYERS
            if layer == N_A_LAYERS:
                kv = _rmsnorm(h, norm_kv) @ w_kv
                ks_, vs_ = jnp.split(kv, 2, axis=-1)
                ks_ = _rope(ks_.reshape(b, s, N_HEADS, HEAD_DIM))
                vs_ = vs_.reshape(b, s, N_HEADS, HEAD_DIM)
                kv_blocks = tuple((_to_blocks(ks_, r), _to_blocks(vs_, r))
                                  for (_, r) in DILATED_BRANCHES)
            q = _rope((hn @ w_q_b[j]).reshape(b, s, N_HEADS, HEAD_DIM))
            o = _dilated_attention(q, kv_blocks).reshape(b, s, d)
            h = h + o @ w_o_b[j]
        h = h + _swiglu(_rmsnorm(h, norm_ffn[layer]), w_gate[layer], w_up[layer], w_down[layer])
    return _rmsnorm(h, norm_final)
```

```python
import functools

import jax
import jax.numpy as jnp
from jax import lax
from jax.experimental import pallas as pl
from jax.experimental.pallas import tpu as pltpu

HEAD_DIM = 64
BLOCK = 128
ROPE_THETA = 10000.0
NORM_EPS = 1e-6
DILATED_BRANCHES = ((128, 1), (512, 4), (2048, 16))

LANES = 128
VMEM_LIMIT_BYTES = 56 * 1024 * 1024
MASK_VALUE = -1e30

_f32 = jnp.float32
_bf16 = jnp.bfloat16


def _rms(x, g):
    return x * lax.rsqrt(jnp.mean(x * x, axis=-1, keepdims=True) + NORM_EPS) * g


def _compiler_params(semantics):
    return pltpu.CompilerParams(dimension_semantics=semantics, vmem_limit_bytes=VMEM_LIMIT_BYTES)


def _norm_proj_kernel(x_ref, g_ref, w_ref, cos_ref, sin_ref, o_ref, *, rope_cols):
    xn = _rms(x_ref[...], g_ref[...]).astype(_bf16)
    n_out = o_ref.shape[1]
    if rope_cols:
        cos = cos_ref[...]
        sin = sin_ref[...]
        first_half = (lax.broadcasted_iota(jnp.int32, cos.shape, 1) % HEAD_DIM) < HEAD_DIM // 2
    for c in range(n_out // LANES):
        cols = slice(c * LANES, (c + 1) * LANES)
        y = jnp.dot(xn, w_ref[:, cols], preferred_element_type=_f32)
        if c * LANES < rope_cols:
            partner = jnp.where(first_half, pltpu.roll(y, LANES - HEAD_DIM // 2, 1),
                                pltpu.roll(y, HEAD_DIM // 2, 1))
            y = y * cos + partner * sin
        o_ref[:, cols] = y.astype(o_ref.dtype)


def _norm_proj(x, g, w, cos, sin, *, rope_cols, out_dtype, seq, tm=512):
    m, d = x.shape
    n_out = w.shape[1]
    pos_blocks = seq // tm
    return pl.pallas_call(
        functools.partial(_norm_proj_kernel, rope_cols=rope_cols),
        grid=(m // tm,),
        in_specs=[
            pl.BlockSpec((tm, d), lambda i: (i, 0)),
            pl.BlockSpec((1, d), lambda i: (0, 0)),
            pl.BlockSpec((d, n_out), lambda i: (0, 0)),
            pl.BlockSpec((tm, LANES), lambda i: (i % pos_blocks, 0)),
            pl.BlockSpec((tm, LANES), lambda i: (i % pos_blocks, 0)),
        ],
        out_specs=pl.BlockSpec((tm, n_out), lambda i: (i, 0)),
        out_shape=jax.ShapeDtypeStruct((m, n_out), out_dtype),
        compiler_params=_compiler_params(("parallel",)),
        name="norm_proj",
    )(x, g, w, cos, sin)


def _stickbreak_kernel(q_ref, k_ref, v_ref, uu_ref, o_ref, acc_ref, carry_ref, *, tq, tk):
    qi = pl.program_id(2)
    lane = lax.broadcasted_iota(jnp.int32, (1, LANES), 1)
    head_lanes = (lane < HEAD_DIM, lane >= HEAD_DIM)
    q = q_ref[...] * jnp.asarray(HEAD_DIM ** -0.5, _bf16)
    q_heads = [jnp.where(hm, q, jnp.zeros_like(q)) for hm in head_lanes]
    acc_ref[...] = jnp.zeros_like(acc_ref)
    carry_ref[...] = jnp.zeros_like(carry_ref)
    uu = uu_ref[...]

    def key_block(jb, masked):
        start = pl.multiple_of(jb * tk, tk)
        k = k_ref[pl.ds(start, tk), :]
        v = v_ref[pl.ds(start, tk), :]
        if masked:
            q_pos = qi * tq + lax.broadcasted_iota(jnp.int32, (tq, tk), 0)
            k_pos = jb * tk + lax.broadcasted_iota(jnp.int32, (tq, tk), 1)
            strict = k_pos < q_pos
        pv = jnp.zeros((tq, LANES), _f32)
        for h in range(2):
            z = lax.dot_general(q_heads[h], k, (((1,), (1,)), ((), ())),
                                preferred_element_type=_f32)
            softplus = jnp.maximum(z, 0.0) + jnp.log(1.0 + jnp.exp(-jnp.abs(z)))
            log_stay = -softplus
            if masked:
                log_stay = jnp.where(strict, log_stay, 0.0)
            hi = log_stay.astype(_bf16)
            lo = (log_stay - hi.astype(_f32)).astype(_bf16)
            sums = jnp.dot(jnp.concatenate([hi, lo], axis=1), uu,
                           preferred_element_type=_f32)
            log_w = (z - softplus) + sums[:, :tk] + carry_ref[h]
            w = jnp.exp(log_w)
            if masked:
                w = jnp.where(strict, w, 0.0)
            v_head = jnp.where(head_lanes[h], v, jnp.zeros_like(v))
            pv = pv + jnp.dot(w.astype(_bf16), v_head, preferred_element_type=_f32)
            carry_ref[h] = carry_ref[h] + sums[:, tk:]
        acc_ref[...] += pv

    n_diag = tq // tk
    first_diag = qi * n_diag
    for d in reversed(range(n_diag)):
        key_block(first_diag + d, masked=True)

    def body(t, carry):
        key_block(first_diag - 1 - t, masked=False)
        return carry

    lax.fori_loop(0, first_diag, body, 0)
    o_ref[...] = acc_ref[...].astype(o_ref.dtype)


def _cumsum_matrix(tk):
    src = jnp.arange(2 * tk)[:, None] % tk
    dst = jnp.arange(2 * tk)[None, :]
    return jnp.where(dst < tk, src > dst, True).astype(_bf16)


def _stickbreak(qkv, *, batch, seq, d_model, tq=256, tk=128):
    pairs = d_model // LANES
    return pl.pallas_call(
        functools.partial(_stickbreak_kernel, tq=tq, tk=tk),
        grid=(batch, pairs, seq // tq),
        in_specs=[
            pl.BlockSpec((None, tq, LANES), lambda b, p, i: (b, i, p)),
            pl.BlockSpec((None, seq, LANES), lambda b, p, i: (b, 0, pairs + p)),
            pl.BlockSpec((None, seq, LANES), lambda b, p, i: (b, 0, 2 * pairs + p)),
            pl.BlockSpec((2 * tk, 2 * tk), lambda b, p, i: (0, 0)),
        ],
        out_specs=pl.BlockSpec((None, tq, LANES), lambda b, p, i: (b, i, p)),
        out_shape=jax.ShapeDtypeStruct((batch, seq, d_model), _bf16),
        scratch_shapes=[pltpu.VMEM((tq, LANES), _f32), pltpu.VMEM((2, tq, tk), _f32)],
        compiler_params=_compiler_params(("parallel", "parallel", "arbitrary")),
        name="stickbreak",
    )(qkv, qkv, qkv, _cumsum_matrix(tk))


def _dilated_kernel(q_ref, k_ref, v_ref, o_ref, m_ref, l_ref, acc_ref, *, tile):
    tile_start = pl.program_id(2) * tile
    lane = lax.broadcasted_iota(jnp.int32, (1, LANES), 1)
    head_lanes = (lane < HEAD_DIM, lane >= HEAD_DIM)
    row = lax.broadcasted_iota(jnp.int32, (BLOCK, 2 * BLOCK), 0)
    col = lax.broadcasted_iota(jnp.int32, (BLOCK, 2 * BLOCK), 1)
    in_prev = col < BLOCK
    band_prev = in_prev & (col >= row)
    band_cur = (~in_prev) & (col - BLOCK <= row)
    scale = jnp.asarray(HEAD_DIM ** -0.5, _bf16)

    for branch, (window, r) in enumerate(DILATED_BRANCHES):
        assert window // r == BLOCK
        blocks_per_class = tile // (BLOCK * r)

        def block(idx, carry, r=r, branch=branch, blocks_per_class=blocks_per_class):
            c = idx // blocks_per_class
            jb = idx % blocks_per_class
            local = jb * (BLOCK * r) + c
            cur = tile_start + local
            prev = cur - BLOCK * r
            has_prev = prev >= 0
            prev = jnp.maximum(prev, 0)
            q_rows = pl.ds(local, BLOCK, stride=r) if r > 1 else pl.ds(local, BLOCK)
            cur_rows = pl.ds(cur, BLOCK, stride=r) if r > 1 else pl.ds(cur, BLOCK)
            prev_rows = pl.ds(prev, BLOCK, stride=r) if r > 1 else pl.ds(prev, BLOCK)
            q = q_ref[q_rows, :].astype(_bf16) * scale
            k = jnp.concatenate([k_ref[prev_rows, :], k_ref[cur_rows, :]], axis=0).astype(_bf16)
            v = jnp.concatenate([v_ref[prev_rows, :], v_ref[cur_rows, :]], axis=0).astype(_bf16)
            valid = (band_prev & has_prev) | band_cur
            m_t = jnp.zeros((BLOCK, LANES), _f32)
            l_t = jnp.zeros((BLOCK, LANES), _f32)
            pv_t = jnp.zeros((BLOCK, LANES), _f32)
            for h in range(2):
                q_head = jnp.where(head_lanes[h], q, jnp.zeros_like(q))
                z = lax.dot_general(q_head, k, (((1,), (1,)), ((), ())),
                                    preferred_element_type=_f32)
                z = jnp.where(valid, z, MASK_VALUE)
                m_h = jnp.max(z, axis=1, keepdims=True)
                p = jnp.exp(z - m_h)
                l_h = jnp.sum(p, axis=1, keepdims=True)
                v_head = jnp.where(head_lanes[h], v, jnp.zeros_like(v))
                pv_t = pv_t + jnp.dot(p.astype(_bf16), v_head, preferred_element_type=_f32)
                m_t = jnp.where(head_lanes[h], m_h, m_t)
                l_t = jnp.where(head_lanes[h], l_h, l_t)
            if branch == 0:
                m_ref[q_rows, :] = m_t
                l_ref[q_rows, :] = l_t
                acc_ref[q_rows, :] = pv_t
            else:
                m_old = m_ref[q_rows, :]
                m_new = jnp.maximum(m_old, m_t)
                a_old = jnp.exp(m_old - m_new)
                a_new = jnp.exp(m_t - m_new)
                m_ref[q_rows, :] = m_new
                l_ref[q_rows, :] = a_old * l_ref[q_rows, :] + a_new * l_t
                acc_ref[q_rows, :] = a_old * acc_ref[q_rows, :] + a_new * pv_t
            return carry

        lax.fori_loop(0, r * blocks_per_class, block, 0)

    o_ref[...] = (acc_ref[...] / l_ref[...]).astype(o_ref.dtype)


def _dilated(q, kv, *, batch, seq, d_model, tile=2048):
    pairs = d_model // LANES
    return pl.pallas_call(
        functools.partial(_dilated_kernel, tile=tile),
        grid=(batch, pairs, seq // tile),
        in_specs=[
            pl.BlockSpec((None, tile, LANES), lambda b, p, i: (b, i, p)),
            pl.BlockSpec((None, seq, LANES), lambda b, p, i: (b, 0, p)),
            pl.BlockSpec((None, seq, LANES), lambda b, p, i: (b, 0, pairs + p)),
        ],
        out_specs=pl.BlockSpec((None, tile, LANES), lambda b, p, i: (b, i, p)),
        out_shape=jax.ShapeDtypeStruct((batch, seq, d_model), _bf16),
        scratch_shapes=[pltpu.VMEM((tile, LANES), _f32)] * 3,
        compiler_params=_compiler_params(("parallel", "parallel", "arbitrary")),
        name="dilated",
    )(q, kv, kv)


def _post_attn_kernel(h_ref, o_ref, wo_ref, g_ref, wg_ref, wu_ref, wd_ref, gf_ref, out_ref, acc_ref,
                      *, tf, final_norm):
    h1 = h_ref[...] + jnp.dot(o_ref[...], wo_ref[...], preferred_element_type=_f32)
    xn = _rms(h1, g_ref[...]).astype(_bf16)
    acc_ref[...] = h1
    for c in range(wg_ref.shape[1] // tf):
        cols = slice(c * tf, (c + 1) * tf)
        gate = jnp.dot(xn, wg_ref[:, cols], preferred_element_type=_f32)
        up = jnp.dot(xn, wu_ref[:, cols], preferred_element_type=_f32)
        act = (gate / (1.0 + jnp.exp(-gate)) * up).astype(_bf16)
        acc_ref[...] += jnp.dot(act, wd_ref[cols, :], preferred_element_type=_f32)
    out = acc_ref[...]
    if final_norm:
        out = _rms(out, gf_ref[...])
    out_ref[...] = out


def _post_attn(h, o, wo, g, wg, wu, wd, g_final, *, final_norm, tm=512, tf=256):
    m, d = h.shape
    f = wg.shape[1]
    resident = lambda shape: pl.BlockSpec(shape, lambda i: (0, 0), pipeline_mode=pl.Buffered(1))
    return pl.pallas_call(
        functools.partial(_post_attn_kernel, tf=tf, final_norm=final_norm),
        grid=(m // tm,),
        in_specs=[
            pl.BlockSpec((tm, d), lambda i: (i, 0)),
            pl.BlockSpec((tm, d), lambda i: (i, 0)),
            resident((d, d)),
            resident((1, d)),
            resident((d, f)),
            resident((d, f)),
            resident((f, d)),
            resident((1, d)),
        ],
        out_specs=pl.BlockSpec((tm, d), lambda i: (i, 0)),
        out_shape=jax.ShapeDtypeStruct((m, d), _f32),
        scratch_shapes=[pltpu.VMEM((tm, d), _f32)],
        compiler_params=_compiler_params(("parallel",)),
        name="post_attn",
    )(h, o, wo, g, wg, wu, wd, g_final)


def _rope_tables(seq):
    half = HEAD_DIM // 2
    inv_freq = ROPE_THETA ** (-jnp.arange(0, HEAD_DIM, 2, dtype=_f32) / HEAD_DIM)
    ang = jnp.arange(seq, dtype=_f32)[:, None] * inv_freq[None, :]
    cos, sin = jnp.cos(ang), jnp.sin(ang)
    reps = LANES // HEAD_DIM
    return (jnp.tile(jnp.concatenate([cos, cos], axis=1), (1, reps)),
            jnp.tile(jnp.concatenate([-sin, sin], axis=1), (1, reps)))


def kernel(x, norm_mix, w_qkv_a, w_o_a, norm_kv, w_kv, w_q_b, w_o_b, norm_ffn, w_gate, w_up, w_down,
           norm_final):
    b, s, d = x.shape
    depth = norm_mix.shape[0]
    n_a = w_qkv_a.shape[0]
    assert d % LANES == 0 and s % 2048 == 0
    cos, sin = _rope_tables(s)
    cast = lambda w: w.astype(_bf16)
    w_qkv_a, w_o_a, w_kv, w_q_b, w_o_b = map(cast, (w_qkv_a, w_o_a, w_kv, w_q_b, w_o_b))
    w_gate, w_up, w_down = map(cast, (w_gate, w_up, w_down))
    g_final = norm_final.reshape(1, d)

    h = x.reshape(b * s, d)
    kv = None
    for layer in range(depth):
        g_mix = norm_mix[layer].reshape(1, d)
        if layer < n_a:
            qkv = _norm_proj(h, g_mix, w_qkv_a[layer], cos, sin, rope_cols=0, out_dtype=_bf16, seq=s)
            o = _stickbreak(qkv.reshape(b, s, 3 * d), batch=b, seq=s, d_model=d)
            w_o = w_o_a[layer]
        else:
            j = layer - n_a
            if kv is None:
                kv = _norm_proj(h, norm_kv.reshape(1, d), w_kv, cos, sin, rope_cols=d, out_dtype=_f32,
                                seq=s).reshape(b, s, 2 * d)
            q = _norm_proj(h, g_mix, w_q_b[j], cos, sin, rope_cols=d, out_dtype=_f32, seq=s)
            o = _dilated(q.reshape(b, s, d), kv, batch=b, seq=s, d_model=d)
            w_o = w_o_b[j]
        h = _post_attn(h, o.reshape(b * s, d), w_o, norm_ffn[layer].reshape(1, d), w_gate[layer],
                       w_up[layer], w_down[layer], g_final, final_norm=(layer == depth - 1))
    return h.reshape(b, s, d)
```

```python
import functools

import jax
import jax.numpy as jnp
from jax import lax
from jax.experimental import pallas as pl
from jax.experimental.pallas import tpu as pltpu

HEAD_DIM = 64
BLOCK = 128
ROPE_THETA = 10000.0
NORM_EPS = 1e-6
DILATED_BRANCHES = ((128, 1), (512, 4), (2048, 16))

LANES = 128
VMEM_LIMIT_BYTES = 56 * 1024 * 1024
MASK_VALUE = -1e30
EXP2_UNDERFLOW = -160.0
LOG2_E = 1.4426950408889634
MXU_WIDTH = 256

_f32 = jnp.float32
_bf16 = jnp.bfloat16


def _rms(x, g):
    return x * lax.rsqrt(jnp.mean(x * x, axis=-1, keepdims=True) + NORM_EPS) * g


def _compiler_params(semantics):
    return pltpu.CompilerParams(dimension_semantics=semantics, vmem_limit_bytes=VMEM_LIMIT_BYTES)


def _norm_proj_kernel(x_ref, g_ref, w_ref, cos_ref, sin_ref, o_ref, *, rope_cols):
    xn = _rms(x_ref[...], g_ref[...]).astype(_bf16)
    n_out = o_ref.shape[1]
    tn = MXU_WIDTH
    if rope_cols:
        reps = tn // LANES
        cos = jnp.concatenate([cos_ref[...]] * reps, axis=1)
        sin = jnp.concatenate([sin_ref[...]] * reps, axis=1)
        first_half = (lax.broadcasted_iota(jnp.int32, cos.shape, 1) % HEAD_DIM) < HEAD_DIM // 2
    for c in range(n_out // tn):
        cols = slice(c * tn, (c + 1) * tn)
        y = jnp.dot(xn, w_ref[:, cols], preferred_element_type=_f32)
        if c * tn < rope_cols:
            partner = jnp.where(first_half, pltpu.roll(y, tn - HEAD_DIM // 2, 1),
                                pltpu.roll(y, HEAD_DIM // 2, 1))
            y = y * cos + partner * sin
        o_ref[:, cols] = y.astype(o_ref.dtype)


def _norm_proj(x, g, w, cos, sin, *, rope_cols, out_dtype, seq, tm=512):
    m, d = x.shape
    n_out = w.shape[1]
    pos_blocks = seq // tm
    return pl.pallas_call(
        functools.partial(_norm_proj_kernel, rope_cols=rope_cols),
        grid=(m // tm,),
        in_specs=[
            pl.BlockSpec((tm, d), lambda i: (i, 0)),
            pl.BlockSpec((1, d), lambda i: (0, 0)),
            pl.BlockSpec((d, n_out), lambda i: (0, 0)),
            pl.BlockSpec((tm, LANES), lambda i: (i % pos_blocks, 0)),
            pl.BlockSpec((tm, LANES), lambda i: (i % pos_blocks, 0)),
        ],
        out_specs=pl.BlockSpec((tm, n_out), lambda i: (i, 0)),
        out_shape=jax.ShapeDtypeStruct((m, n_out), out_dtype),
        compiler_params=_compiler_params(("parallel",)),
        name="norm_proj",
    )(x, g, w, cos, sin)


def _stickbreak_kernel(q_ref, k_ref, v_ref, uu_ref, o_ref, acc_ref, carry_ref, *, tq, tk):
    qi = pl.program_id(2)
    lane = lax.broadcasted_iota(jnp.int32, (1, LANES), 1)
    head_lanes = (lane < HEAD_DIM, lane >= HEAD_DIM)
    q = q_ref[...] * jnp.asarray(HEAD_DIM ** -0.5, _bf16)
    q_heads = [jnp.where(hm, q, jnp.zeros_like(q)) for hm in head_lanes]
    uu = uu_ref[...]
    strict = (lax.broadcasted_iota(jnp.int32, (tk, tk), 1)
              < lax.broadcasted_iota(jnp.int32, (tk, tk), 0))
    sign_bit = jnp.uint32(0x80000000)

    def mask_top(a, diagonal):
        if not diagonal:
            return a
        top = jnp.where(strict, a[:tk], 0.0)
        return top if a.shape[0] == tk else jnp.concatenate([top, a[tk:]], axis=0)

    def add_rows(x, r0, delta):
        return x + delta if r0 == 0 else jnp.concatenate([x[:r0], x[r0:] + delta], axis=0)

    def key_blocks(blocks, state):
        carries, acc = list(state[:2]), state[2]
        keys, values, logits, afters = [], [], {}, {}
        for jb, r0, _ in blocks:
            start = pl.multiple_of(jb * tk, tk)
            k = k_ref[pl.ds(start, tk), :]
            v = v_ref[pl.ds(start, tk), :]
            keys.append(k)
            values.append(jnp.concatenate(
                [jnp.where(hm, v, jnp.zeros_like(v)) for hm in head_lanes], axis=0))
        for b, (_, r0, _) in enumerate(blocks):
            for h in range(2):
                logits[b, h] = lax.dot_general(q_heads[h][r0:], keys[b], (((1,), (1,)), ((), ())),
                                               preferred_element_type=_f32) * LOG2_E
        for b, (_, r0, diagonal) in enumerate(blocks):
            for h in range(2):
                z = logits[b, h]
                neg_abs = lax.bitcast_convert_type(lax.bitcast_convert_type(z, jnp.uint32) | sign_bit, _f32)
                softplus = jnp.maximum(z, 0.0) + jnp.log2(1.0 + jnp.exp2(neg_abs))
                sp = mask_top(softplus, diagonal)
                hi = sp.astype(_bf16)
                lo = (sp - hi.astype(_f32)).astype(_bf16)
                afters[b, h] = jnp.dot(jnp.concatenate([hi, lo], axis=1), uu,
                                       preferred_element_type=_f32)
        for b, (_, r0, diagonal) in enumerate(blocks):
            weights = []
            for h in range(2):
                after = afters[b, h]
                w = mask_top(jnp.exp2(logits[b, h] + after + carries[h][r0:]), diagonal)
                weights.append(w.astype(_bf16))
                carries[h] = add_rows(carries[h], r0, jnp.broadcast_to(after[:, :1], after.shape))
            pv = jnp.dot(jnp.concatenate(weights, axis=1), values[b], preferred_element_type=_f32)
            acc = add_rows(acc, r0, pv)
        return carries[0], carries[1], acc

    blocks_per_tile = tq // tk
    first_diag = qi * blocks_per_tile

    def near_keys(with_previous_tile):
        zero = jnp.zeros((tq, LANES), _f32)
        blocks = [(first_diag + d, d * tk, True) for d in reversed(range(blocks_per_tile))]
        if with_previous_tile:
            blocks += [(first_diag - 1 - d, 0, False) for d in range(blocks_per_tile)]
        return key_blocks(blocks, (zero, zero, zero))

    def carry_max(state):
        return jnp.max(jnp.maximum(state[0], state[1]))

    def save(state):
        carry_ref[0], carry_ref[1], acc_ref[...] = state
        return carry_max(state)

    c_max = lax.cond(qi > 0, lambda: save(near_keys(True)), lambda: save(near_keys(False)))

    def more(loop):
        t, c_max = loop
        return (t < qi) & (c_max > EXP2_UNDERFLOW)

    def older_keys(loop):
        t, _ = loop
        blocks = [(first_diag - 1 - t * blocks_per_tile - d, 0, False) for d in range(blocks_per_tile)]
        return t + 1, save(key_blocks(blocks, (carry_ref[0], carry_ref[1], acc_ref[...])))

    lax.while_loop(more, older_keys, (jnp.int32(1), c_max))
    o_ref[...] = acc_ref[...].astype(o_ref.dtype)


def _cumsum_matrix(tk):
    src = jnp.arange(2 * tk)[:, None] % tk
    dst = jnp.arange(tk)[None, :]
    return -(src >= dst).astype(_bf16)


def _stickbreak(qkv, *, batch, seq, d_model, tq=256, tk=128):
    pairs = d_model // LANES
    return pl.pallas_call(
        functools.partial(_stickbreak_kernel, tq=tq, tk=tk),
        grid=(batch, pairs, seq // tq),
        in_specs=[
            pl.BlockSpec((None, tq, LANES), lambda b, p, i: (b, i, p)),
            pl.BlockSpec((None, seq, LANES), lambda b, p, i: (b, 0, pairs + p)),
            pl.BlockSpec((None, seq, LANES), lambda b, p, i: (b, 0, 2 * pairs + p)),
            pl.BlockSpec((2 * tk, tk), lambda b, p, i: (0, 0)),
        ],
        out_specs=pl.BlockSpec((None, tq, LANES), lambda b, p, i: (b, i, p)),
        out_shape=jax.ShapeDtypeStruct((batch, seq, d_model), _bf16),
        scratch_shapes=[pltpu.VMEM((tq, LANES), _f32), pltpu.VMEM((2, tq, tk), _f32)],
        compiler_params=_compiler_params(("parallel", "parallel", "arbitrary")),
        name="stickbreak",
    )(qkv, qkv, qkv, _cumsum_matrix(tk))


def _dilated_kernel(q_ref, k_ref, v_ref, o_ref, m_ref, l_ref, acc_ref, *, tile, group):
    tile_start = pl.program_id(2) * tile
    lane = lax.broadcasted_iota(jnp.int32, (1, LANES), 1)
    head_lanes = (lane < HEAD_DIM, lane >= HEAD_DIM)
    row = lax.broadcasted_iota(jnp.int32, (BLOCK, 2 * BLOCK), 0)
    col = lax.broadcasted_iota(jnp.int32, (BLOCK, 2 * BLOCK), 1)
    in_prev = col < BLOCK
    band_prev = in_prev & (col >= row)
    band_cur = (~in_prev) & (col - BLOCK <= row)
    scale = jnp.asarray(HEAD_DIM ** -0.5, _bf16)

    for branch, (window, r) in enumerate(DILATED_BRANCHES):
        assert window // r == BLOCK
        blocks_per_class = tile // (BLOCK * r)

        def block_group(g, carry, r=r, branch=branch, blocks_per_class=blocks_per_class):
            rows, ks, vs, valids, logits, probs, maxes, sums = [], [], [], [], {}, {}, {}, {}
            for u in range(group):
                idx = g * group + u
                c = idx // blocks_per_class
                jb = idx % blocks_per_class
                local = jb * (BLOCK * r) + c
                cur = tile_start + local
                prev = cur - BLOCK * r
                has_prev = prev >= 0
                prev = jnp.maximum(prev, 0)
                q_rows = pl.ds(local, BLOCK, stride=r) if r > 1 else pl.ds(local, BLOCK)
                cur_rows = pl.ds(cur, BLOCK, stride=r) if r > 1 else pl.ds(cur, BLOCK)
                prev_rows = pl.ds(prev, BLOCK, stride=r) if r > 1 else pl.ds(prev, BLOCK)
                q = q_ref[q_rows, :].astype(_bf16) * scale
                k = jnp.concatenate([k_ref[prev_rows, :], k_ref[cur_rows, :]], axis=0).astype(_bf16)
                v = jnp.concatenate([v_ref[prev_rows, :], v_ref[cur_rows, :]], axis=0).astype(_bf16)
                rows.append(q_rows)
                ks.append(k)
                vs.append(jnp.concatenate(
                    [jnp.where(hm, v, jnp.zeros_like(v)) for hm in head_lanes], axis=0))
                valids.append((band_prev & has_prev) | band_cur)
                for h in range(2):
                    q_head = jnp.where(head_lanes[h], q, jnp.zeros_like(q))
                    logits[u, h] = lax.dot_general(q_head, k, (((1,), (1,)), ((), ())),
                                                   preferred_element_type=_f32)
            for u in range(group):
                for h in range(2):
                    z = jnp.where(valids[u], logits[u, h], MASK_VALUE)
                    maxes[u, h] = jnp.max(z, axis=1, keepdims=True)
                    p = jnp.exp(z - maxes[u, h])
                    sums[u, h] = jnp.sum(p, axis=1, keepdims=True)
                    probs[u, h] = p.astype(_bf16)
            for u in range(group):
                q_rows = rows[u]
                pv_t = jnp.dot(jnp.concatenate([probs[u, 0], probs[u, 1]], axis=1), vs[u],
                               preferred_element_type=_f32)
                m_t = jnp.where(head_lanes[0], maxes[u, 0], maxes[u, 1])
                l_t = jnp.where(head_lanes[0], sums[u, 0], sums[u, 1])
                if branch == 0:
                    m_ref[q_rows, :] = m_t
                    l_ref[q_rows, :] = l_t
                    acc_ref[q_rows, :] = pv_t
                else:
                    m_old = m_ref[q_rows, :]
                    m_new = jnp.maximum(m_old, m_t)
                    a_old = jnp.exp(m_old - m_new)
                    a_new = jnp.exp(m_t - m_new)
                    m_ref[q_rows, :] = m_new
                    l_ref[q_rows, :] = a_old * l_ref[q_rows, :] + a_new * l_t
                    acc_ref[q_rows, :] = a_old * acc_ref[q_rows, :] + a_new * pv_t
            return carry

        lax.fori_loop(0, r * blocks_per_class // group, block_group, 0)

    o_ref[...] = (acc_ref[...] / l_ref[...]).astype(o_ref.dtype)


def _dilated(q, kv, *, batch, seq, d_model, tile=2048, group=4):
    pairs = d_model // LANES
    assert (tile // BLOCK) % group == 0
    return pl.pallas_call(
        functools.partial(_dilated_kernel, tile=tile, group=group),
        grid=(batch, pairs, seq // tile),
        in_specs=[
            pl.BlockSpec((None, tile, LANES), lambda b, p, i: (b, i, p)),
            pl.BlockSpec((None, seq, LANES), lambda b, p, i: (b, 0, p)),
            pl.BlockSpec((None, seq, LANES), lambda b, p, i: (b, 0, pairs + p)),
        ],
        out_specs=pl.BlockSpec((None, tile, LANES), lambda b, p, i: (b, i, p)),
        out_shape=jax.ShapeDtypeStruct((batch, seq, d_model), _bf16),
        scratch_shapes=[pltpu.VMEM((tile, LANES), _f32)] * 3,
        compiler_params=_compiler_params(("parallel", "parallel", "arbitrary")),
        name="dilated",
    )(q, kv, kv)


def _post_attn_kernel(h_ref, o_ref, wo_ref, g_ref, wg_ref, wu_ref, wd_ref, gf_ref, out_ref, acc_ref,
                      *, tf, final_norm):
    h1 = h_ref[...] + jnp.dot(o_ref[...], wo_ref[...], preferred_element_type=_f32)
    xn = _rms(h1, g_ref[...]).astype(_bf16)
    acc_ref[...] = h1
    for c in range(wg_ref.shape[1] // tf):
        cols = slice(c * tf, (c + 1) * tf)
        gate = jnp.dot(xn, wg_ref[:, cols], preferred_element_type=_f32)
        up = jnp.dot(xn, wu_ref[:, cols], preferred_element_type=_f32)
        act = (gate / (1.0 + jnp.exp(-gate)) * up).astype(_bf16)
        acc_ref[...] += jnp.dot(act, wd_ref[cols, :], preferred_element_type=_f32)
    out = acc_ref[...]
    if final_norm:
        out = _rms(out, gf_ref[...])
    out_ref[...] = out


def _post_attn(h, o, wo, g, wg, wu, wd, g_final, *, final_norm, tm=512, tf=256):
    m, d = h.shape
    f = wg.shape[1]
    resident = lambda shape: pl.BlockSpec(shape, lambda i: (0, 0), pipeline_mode=pl.Buffered(1))
    return pl.pallas_call(
        functools.partial(_post_attn_kernel, tf=tf, final_norm=final_norm),
        grid=(m // tm,),
        in_specs=[
            pl.BlockSpec((tm, d), lambda i: (i, 0)),
            pl.BlockSpec((tm, d), lambda i: (i, 0)),
            resident((d, d)),
            resident((1, d)),
            resident((d, f)),
            resident((d, f)),
            resident((f, d)),
            resident((1, d)),
        ],
        out_specs=pl.BlockSpec((tm, d), lambda i: (i, 0)),
        out_shape=jax.ShapeDtypeStruct((m, d), _f32),
        scratch_shapes=[pltpu.VMEM((tm, d), _f32)],
        compiler_params=_compiler_params(("parallel",)),
        name="post_attn",
    )(h, o, wo, g, wg, wu, wd, g_final)


def _rope_tables(seq):
    half = HEAD_DIM // 2
    inv_freq = ROPE_THETA ** (-jnp.arange(0, HEAD_DIM, 2, dtype=_f32) / HEAD_DIM)
    ang = jnp.arange(seq, dtype=_f32)[:, None] * inv_freq[None, :]
    cos, sin = jnp.cos(ang), jnp.sin(ang)
    reps = LANES // HEAD_DIM
    return (jnp.tile(jnp.concatenate([cos, cos], axis=1), (1, reps)),
            jnp.tile(jnp.concatenate([-sin, sin], axis=1), (1, reps)))


def kernel(x, norm_mix, w_qkv_a, w_o_a, norm_kv, w_kv, w_q_b, w_o_b, norm_ffn, w_gate, w_up, w_down,
           norm_final):
    b, s, d = x.shape
    depth = norm_mix.shape[0]
    n_a = w_qkv_a.shape[0]
    assert d % LANES == 0 and s % 2048 == 0
    cos, sin = _rope_tables(s)
    cast = lambda w: w.astype(_bf16)
    w_qkv_a, w_o_a, w_kv, w_q_b, w_o_b = map(cast, (w_qkv_a, w_o_a, w_kv, w_q_b, w_o_b))
    w_gate, w_up, w_down = map(cast, (w_gate, w_up, w_down))
    g_final = norm_final.reshape(1, d)

    h = x.reshape(b * s, d)
    kv = None
    for layer in range(depth):
        g_mix = norm_mix[layer].reshape(1, d)
        if layer < n_a:
            qkv = _norm_proj(h, g_mix, w_qkv_a[layer], cos, sin, rope_cols=0, out_dtype=_bf16, seq=s)
            o = _stickbreak(qkv.reshape(b, s, 3 * d), batch=b, seq=s, d_model=d)
            w_o = w_o_a[layer]
        else:
            j = layer - n_a
            if kv is None:
                kv = _norm_proj(h, norm_kv.reshape(1, d), w_kv, cos, sin, rope_cols=d, out_dtype=_f32,
                                seq=s).reshape(b, s, 2 * d)
            q = _norm_proj(h, g_mix, w_q_b[j], cos, sin, rope_cols=d, out_dtype=_f32, seq=s)
            o = _dilated(q.reshape(b, s, d), kv, batch=b, seq=s, d_model=d)
            w_o = w_o_b[j]
        h = _post_attn(h, o.reshape(b * s, d), w_o, norm_ffn[layer].reshape(1, d), w_gate[layer],
                       w_up[layer], w_down[layer], g_final, final_norm=(layer == depth - 1))
    return h.reshape(b, s, d)
```

```python
import functools

import jax
import jax.numpy as jnp
from jax import lax
from jax.experimental import pallas as pl
from jax.experimental.pallas import tpu as pltpu

HEAD_DIM = 64
BLOCK = 128
ROPE_THETA = 10000.0
NORM_EPS = 1e-6
DILATED_BRANCHES = ((128, 1), (512, 4), (2048, 16))
N_CLASSES = max(r for _, r in DILATED_BRANCHES)

LANES = 128
VMEM_LIMIT_BYTES = 56 * 1024 * 1024
MASK_VALUE = -1e30
EXP2_UNDERFLOW = -160.0
LOG2_E = 1.4426950408889634
NEAR_KEYS = 192
MXU_WIDTH = 256

_f32 = jnp.float32
_bf16 = jnp.bfloat16


def _rms(x, g):
    return x * lax.rsqrt(jnp.mean(x * x, axis=-1, keepdims=True) + NORM_EPS) * g


def _compiler_params(semantics):
    return pltpu.CompilerParams(dimension_semantics=semantics, vmem_limit_bytes=VMEM_LIMIT_BYTES)


def _head_lanes():
    lane = lax.broadcasted_iota(jnp.int32, (1, LANES), 1)
    return lane < HEAD_DIM, lane >= HEAD_DIM


def _by_head(x, head_lanes):
    return jnp.concatenate([jnp.where(hm, x, jnp.zeros_like(x)) for hm in head_lanes], axis=0)


def _norm_proj_kernel(x_ref, g_ref, w_ref, cos_ref, sin_ref, o_ref, *, rope_cols):
    xn = _rms(x_ref[...], g_ref[...]).astype(_bf16)
    n_out = o_ref.shape[1]
    tn = MXU_WIDTH
    if rope_cols:
        reps = tn // LANES
        cos = jnp.concatenate([cos_ref[...]] * reps, axis=1)
        sin = jnp.concatenate([sin_ref[...]] * reps, axis=1)
        first_half = (lax.broadcasted_iota(jnp.int32, cos.shape, 1) % HEAD_DIM) < HEAD_DIM // 2
    for c in range(n_out // tn):
        cols = slice(c * tn, (c + 1) * tn)
        y = jnp.dot(xn, w_ref[:, cols], preferred_element_type=_f32)
        if c * tn < rope_cols:
            partner = jnp.where(first_half, pltpu.roll(y, tn - HEAD_DIM // 2, 1),
                                pltpu.roll(y, HEAD_DIM // 2, 1))
            y = y * cos + partner * sin
        o_ref[:, cols] = y.astype(o_ref.dtype)


def _norm_proj(x, g, w, cos, sin, *, rope_cols, out_dtype, seq, tm=512):
    m, d = x.shape
    n_out = w.shape[1]
    pos_blocks = seq // tm
    return pl.pallas_call(
        functools.partial(_norm_proj_kernel, rope_cols=rope_cols),
        grid=(m // tm,),
        in_specs=[
            pl.BlockSpec((tm, d), lambda i: (i, 0)),
            pl.BlockSpec((1, d), lambda i: (0, 0)),
            pl.BlockSpec((d, n_out), lambda i: (0, 0)),
            pl.BlockSpec((tm, LANES), lambda i: (i % pos_blocks, 0)),
            pl.BlockSpec((tm, LANES), lambda i: (i % pos_blocks, 0)),
        ],
        out_specs=pl.BlockSpec((tm, n_out), lambda i: (i, 0)),
        out_shape=jax.ShapeDtypeStruct((m, n_out), out_dtype),
        compiler_params=_compiler_params(("parallel",)),
        name="norm_proj",
    )(x, g, w, cos, sin)


def _stickbreak_kernel(q_ref, k_ref, v_ref, uu_ref, o_ref, acc_ref, carry_ref, *, tq, tk):
    qi = pl.program_id(2)
    head_lanes = _head_lanes()
    q = q_ref[...] * jnp.asarray(HEAD_DIM ** -0.5, _bf16)
    uu = uu_ref[...]
    strict = (lax.broadcasted_iota(jnp.int32, (tk, tk), 1)
              < lax.broadcasted_iota(jnp.int32, (tk, tk), 0))
    sign_bit = jnp.uint32(0x80000000)

    def mask_top(a, diagonal):
        if not diagonal:
            return a
        top = jnp.where(strict, a[:tk], 0.0)
        return top if a.shape[0] == tk else jnp.concatenate([top, a[tk:]], axis=0)

    def add_rows(x, r0, r1, delta):
        parts = ([x[:r0]] if r0 else []) + [x[r0:r1] + delta] + ([x[r1:]] if r1 < tq else [])
        return parts[0] if len(parts) == 1 else jnp.concatenate(parts, axis=0)

    def key_blocks(blocks, state):
        carries, acc = list(state[:2]), state[2]
        values, logits, afters = [], {}, {}
        for b, (jb, r0, r1, _) in enumerate(blocks):
            start = pl.multiple_of(jb * tk, tk)
            z = lax.dot_general(q[r0:r1], _by_head(k_ref[pl.ds(start, tk), :], head_lanes),
                                (((1,), (1,)), ((), ())), preferred_element_type=_f32)
            logits[b, 0], logits[b, 1] = z[:, :tk] * LOG2_E, z[:, tk:] * LOG2_E
            values.append(_by_head(v_ref[pl.ds(start, tk), :], head_lanes))
        for b, (_, _, _, diagonal) in enumerate(blocks):
            for h in range(2):
                z = logits[b, h]
                neg_abs = lax.bitcast_convert_type(lax.bitcast_convert_type(z, jnp.uint32) | sign_bit, _f32)
                softplus = jnp.maximum(z, 0.0) + jnp.log2(1.0 + jnp.exp2(neg_abs))
                sp = mask_top(softplus, diagonal)
                hi = sp.astype(_bf16)
                lo = (sp - hi.astype(_f32)).astype(_bf16)
                afters[b, h] = jnp.dot(jnp.concatenate([hi, lo], axis=1), uu,
                                       preferred_element_type=_f32)
        for b, (_, r0, r1, diagonal) in enumerate(blocks):
            weights = []
            for h in range(2):
                after = afters[b, h]
                w = mask_top(jnp.exp2(logits[b, h] + after + carries[h][r0:r1]), diagonal)
                weights.append(w.astype(_bf16))
                carries[h] = add_rows(carries[h], r0, r1, jnp.broadcast_to(after[:, :1], after.shape))
            pv = jnp.dot(jnp.concatenate(weights, axis=1), values[b], preferred_element_type=_f32)
            acc = add_rows(acc, r0, r1, pv)
        return carries[0], carries[1], acc

    blocks_per_tile = tq // tk
    first_diag = qi * blocks_per_tile

    def near_keys(with_previous_tile):
        zero = jnp.zeros((tq, LANES), _f32)
        blocks = [(first_diag + d, d * tk, tq, True) for d in reversed(range(blocks_per_tile))]
        if with_previous_tile:
            blocks += [(first_diag - 1 - d, 0, skipped_from[d], False)
                       for d in range(blocks_per_tile) if skipped_from[d]]
        return key_blocks(blocks, (zero, zero, zero))

    skipped_from = [min(tq, max(NEAR_KEYS - d * tk, 0)) for d in range(blocks_per_tile)]
    partly_skipped = [d for d in range(blocks_per_tile) if skipped_from[d] < tq]

    def carry_max(state, r0=0):
        return jnp.max(jnp.maximum(state[0][r0:], state[1][r0:]))

    def save(state):
        carry_ref[0], carry_ref[1], acc_ref[...] = state
        return state

    def load():
        return carry_ref[0], carry_ref[1], acc_ref[...]

    def first_tile():
        c_max = carry_max(save(near_keys(False)))
        return (c_max,) * (1 + len(partly_skipped))

    def later_tile():
        state = save(near_keys(True))
        return (carry_max(state),) + tuple(carry_max(state, skipped_from[d]) for d in partly_skipped)

    c_max, *c_max_skipped = lax.cond(qi > 0, later_tile, first_tile)

    for d, c_max_rows in zip(partly_skipped, c_max_skipped):
        @pl.when((qi > 0) & (c_max_rows > EXP2_UNDERFLOW))
        def _(d=d):
            save(key_blocks([(first_diag - 1 - d, skipped_from[d], tq, False)], load()))

    def more(loop):
        t, c_max = loop
        return (t < qi) & (c_max > EXP2_UNDERFLOW)

    def older_keys(loop):
        t, _ = loop
        blocks = [(first_diag - 1 - t * blocks_per_tile - d, 0, tq, False) for d in range(blocks_per_tile)]
        return t + 1, carry_max(save(key_blocks(blocks, load())))

    lax.while_loop(more, older_keys, (jnp.int32(1), c_max))
    o_ref[...] = acc_ref[...].astype(o_ref.dtype)


def _cumsum_matrix(tk):
    src = jnp.arange(2 * tk)[:, None] % tk
    dst = jnp.arange(tk)[None, :]
    return -(src >= dst).astype(_bf16)


def _stickbreak(qkv, *, batch, seq, d_model, tq=256, tk=128):
    pairs = d_model // LANES
    return pl.pallas_call(
        functools.partial(_stickbreak_kernel, tq=tq, tk=tk),
        grid=(batch, pairs, seq // tq),
        in_specs=[
            pl.BlockSpec((None, tq, LANES), lambda b, p, i: (b, i, p)),
            pl.BlockSpec((None, seq, LANES), lambda b, p, i: (b, 0, pairs + p)),
            pl.BlockSpec((None, seq, LANES), lambda b, p, i: (b, 0, 2 * pairs + p)),
            pl.BlockSpec((2 * tk, tk), lambda b, p, i: (0, 0)),
        ],
        out_specs=pl.BlockSpec((None, tq, LANES), lambda b, p, i: (b, i, p)),
        out_shape=jax.ShapeDtypeStruct((batch, seq, d_model), _bf16),
        scratch_shapes=[pltpu.VMEM((tq, LANES), _f32), pltpu.VMEM((2, tq, tk), _f32)],
        compiler_params=_compiler_params(("parallel", "parallel", "arbitrary")),
        name="stickbreak",
    )(qkv, qkv, qkv, _cumsum_matrix(tk))


def _dilated_kernel(q_ref, k_ref, v_ref, o_ref, m_ref, l_ref, acc_ref, *, group):
    per_class = q_ref.shape[1]
    tile_start = pl.program_id(2) * per_class
    head_lanes = _head_lanes()
    scale = jnp.asarray(HEAD_DIM ** -0.5, _bf16)
    row = lax.broadcasted_iota(jnp.int32, (BLOCK, 2 * BLOCK), 0)
    col = lax.broadcasted_iota(jnp.int32, (BLOCK, 2 * BLOCK), 1)
    in_prev = col < BLOCK
    ones_by_head = ((lax.broadcasted_iota(jnp.int32, (4 * BLOCK, LANES), 0) < 2 * BLOCK)
                    == (lax.broadcasted_iota(jnp.int32, (4 * BLOCK, LANES), 1) < HEAD_DIM)
                    ).astype(_f32).astype(_bf16)

    for branch, (window, r) in enumerate(DILATED_BRANCHES):
        assert window // r == BLOCK
        n_sub = N_CLASSES // r
        chunk = BLOCK // n_sub
        blocks_per_class = per_class // chunk
        sub_index = lambda a: n_sub * (a % chunk) + a // chunk
        q_sub, k_sub = sub_index(row), sub_index(col % BLOCK)
        band_prev = in_prev & (k_sub >= q_sub)
        band_cur = (~in_prev) & (k_sub <= q_sub)

        def block_group(g, carry, r=r, branch=branch, n_sub=n_sub, chunk=chunk,
                        blocks_per_class=blocks_per_class, band_prev=band_prev, band_cur=band_cur):
            places, vs, valids, logits, probs, maxes = [], [], [], {}, {}, {}
            for u in range(group):
                idx = g * group + u
                c_r = idx // blocks_per_class
                local = pl.multiple_of((idx % blocks_per_class) * chunk, chunk)
                cur = tile_start + local
                has_prev = cur >= chunk
                prev = jnp.maximum(cur - chunk, 0)
                classes = [c_r + r * m for m in range(n_sub)]

                def rows(ref, start):
                    return jnp.concatenate([ref[c, pl.ds(start, chunk), :] for c in classes], axis=0)

                q = rows(q_ref, local).astype(_bf16) * scale
                k = jnp.concatenate([rows(k_ref, prev), rows(k_ref, cur)], axis=0).astype(_bf16)
                v = jnp.concatenate([rows(v_ref, prev), rows(v_ref, cur)], axis=0).astype(_bf16)
                places.append((classes, local))
                vs.append(jnp.concatenate([_by_head(v, head_lanes), ones_by_head], axis=1))
                valids.append((band_prev & has_prev) | band_cur)
                for h in range(2):
                    q_head = jnp.where(head_lanes[h], q, jnp.zeros_like(q))
                    logits[u, h] = lax.dot_general(q_head, k, (((1,), (1,)), ((), ())),
                                                   preferred_element_type=_f32)
            for u in range(group):
                for h in range(2):
                    z = jnp.where(valids[u], logits[u, h], MASK_VALUE)
                    maxes[u, h] = jnp.max(z, axis=1, keepdims=True)
                    probs[u, h] = jnp.exp(z - maxes[u, h]).astype(_bf16)
            for u in range(group):
                classes, local = places[u]
                pv_l = jnp.dot(jnp.concatenate([probs[u, 0], probs[u, 1]], axis=1), vs[u],
                               preferred_element_type=_f32)
                pv_t, l_t = pv_l[:, :LANES], pv_l[:, LANES:]
                m_t = jnp.where(head_lanes[0], maxes[u, 0], maxes[u, 1])
                for m, c in enumerate(classes):
                    part = slice(m * chunk, (m + 1) * chunk)
                    place = (c, pl.ds(local, chunk), slice(None))
                    if branch == 0:
                        m_ref[place] = m_t[part]
                        l_ref[place] = l_t[part]
                        acc_ref[place] = pv_t[part]
                    else:
                        m_old = m_ref[place]
                        m_new = jnp.maximum(m_old, m_t[part])
                        a_old = jnp.exp(m_old - m_new)
                        a_new = jnp.exp(m_t[part] - m_new)
                        m_ref[place] = m_new
                        l_ref[place] = a_old * l_ref[place] + a_new * l_t[part]
                        acc_ref[place] = a_old * acc_ref[place] + a_new * pv_t[part]
            return carry

        lax.fori_loop(0, r * blocks_per_class // group, block_group, 0)

    o_ref[...] = (acc_ref[...] / l_ref[...]).astype(o_ref.dtype)


def _dilated(q, kv, *, batch, seq, d_model, group=4):
    pairs = d_model // LANES
    per_class = seq // N_CLASSES
    assert per_class % BLOCK == 0 and N_CLASSES % group == 0
    return pl.pallas_call(
        functools.partial(_dilated_kernel, group=group),
        grid=(batch, pairs, per_class // BLOCK),
        in_specs=[
            pl.BlockSpec((None, N_CLASSES, BLOCK, LANES), lambda b, p, i: (b, 0, i, p)),
            pl.BlockSpec((None, N_CLASSES, per_class, LANES), lambda b, p, i: (b, 0, 0, p)),
            pl.BlockSpec((None, N_CLASSES, per_class, LANES), lambda b, p, i: (b, 0, 0, pairs + p)),
        ],
        out_specs=pl.BlockSpec((None, N_CLASSES, BLOCK, LANES), lambda b, p, i: (b, 0, i, p)),
        out_shape=jax.ShapeDtypeStruct((batch, N_CLASSES, per_class, d_model), _bf16),
        scratch_shapes=[pltpu.VMEM((N_CLASSES, BLOCK, LANES), _f32)] * 3,
        compiler_params=_compiler_params(("parallel", "parallel", "arbitrary")),
        name="dilated",
    )(q, kv, kv)


def _post_attn_kernel(h_ref, o_ref, wo_ref, g_ref, wg_ref, wu_ref, wd_ref, gf_ref, out_ref, acc_ref,
                      *, tf, final_norm):
    h1 = h_ref[...] + jnp.dot(o_ref[...], wo_ref[...], preferred_element_type=_f32)
    xn = _rms(h1, g_ref[...]).astype(_bf16)
    acc_ref[...] = h1
    for c in range(wg_ref.shape[1] // tf):
        cols = slice(c * tf, (c + 1) * tf)
        gate = jnp.dot(xn, wg_ref[:, cols], preferred_element_type=_f32)
        up = jnp.dot(xn, wu_ref[:, cols], preferred_element_type=_f32)
        act = (gate / (1.0 + jnp.exp(-gate)) * up).astype(_bf16)
        acc_ref[...] += jnp.dot(act, wd_ref[cols, :], preferred_element_type=_f32)
    out = acc_ref[...]
    if final_norm:
        out = _rms(out, gf_ref[...])
    out_ref[...] = out


def _post_attn(h, o, wo, g, wg, wu, wd, g_final, *, final_norm, tm=512, tf=256):
    m, d = h.shape
    f = wg.shape[1]
    resident = lambda shape: pl.BlockSpec(shape, lambda i: (0, 0), pipeline_mode=pl.Buffered(1))
    return pl.pallas_call(
        functools.partial(_post_attn_kernel, tf=tf, final_norm=final_norm),
        grid=(m // tm,),
        in_specs=[
            pl.BlockSpec((tm, d), lambda i: (i, 0)),
            pl.BlockSpec((tm, d), lambda i: (i, 0)),
            resident((d, d)),
            resident((1, d)),
            resident((d, f)),
            resident((d, f)),
            resident((f, d)),
            resident((1, d)),
        ],
        out_specs=pl.BlockSpec((tm, d), lambda i: (i, 0)),
        out_shape=jax.ShapeDtypeStruct((m, d), _f32),
        scratch_shapes=[pltpu.VMEM((tm, d), _f32)],
        compiler_params=_compiler_params(("parallel",)),
        name="post_attn",
    )(h, o, wo, g, wg, wu, wd, g_final)


def _rope_tables(seq):
    inv_freq = ROPE_THETA ** (-jnp.arange(0, HEAD_DIM, 2, dtype=_f32) / HEAD_DIM)
    ang = jnp.arange(seq, dtype=_f32)[:, None] * inv_freq[None, :]
    cos, sin = jnp.cos(ang), jnp.sin(ang)
    reps = LANES // HEAD_DIM
    return (jnp.tile(jnp.concatenate([cos, cos], axis=1), (1, reps)),
            jnp.tile(jnp.concatenate([-sin, sin], axis=1), (1, reps)))


def _class_major(a, lead):
    s, f = a.shape[-2:]
    return jnp.swapaxes(a.reshape(*lead, s // N_CLASSES, N_CLASSES, f), -3, -2)


def kernel(x, norm_mix, w_qkv_a, w_o_a, norm_kv, w_kv, w_q_b, w_o_b, norm_ffn, w_gate, w_up, w_down,
           norm_final):
    b, s, d = x.shape
    depth = norm_mix.shape[0]
    n_a = w_qkv_a.shape[0]
    assert d % LANES == 0 and s % (N_CLASSES * BLOCK) == 0
    cos, sin = _rope_tables(s)
    cos_cm, sin_cm = (_class_major(t, ()).reshape(s, LANES) for t in (cos, sin))
    cast = lambda w: w.astype(_bf16)
    w_qkv_a, w_o_a, w_kv, w_q_b, w_o_b = map(cast, (w_qkv_a, w_o_a, w_kv, w_q_b, w_o_b))
    w_gate, w_up, w_down = map(cast, (w_gate, w_up, w_down))
    g_final = norm_final.reshape(1, d)
    per_class = s // N_CLASSES

    h = x.reshape(b * s, d)
    kv = None
    for layer in range(depth):
        g_mix = norm_mix[layer].reshape(1, d)
        if layer < n_a:
            qkv = _norm_proj(h, g_mix, w_qkv_a[layer], cos, sin, rope_cols=0, out_dtype=_bf16, seq=s)
            o = _stickbreak(qkv.reshape(b, s, 3 * d), batch=b, seq=s, d_model=d)
            w_o = w_o_a[layer]
        else:
            j = layer - n_a
            if kv is None:
                h = _class_major(h.reshape(b, s, d), (b,)).reshape(b * s, d)
                kv = _norm_proj(h, norm_kv.reshape(1, d), w_kv, cos_cm, sin_cm, rope_cols=d, out_dtype=_f32,
                                seq=s).reshape(b, N_CLASSES, per_class, 2 * d)
            q = _norm_proj(h, g_mix, w_q_b[j], cos_cm, sin_cm, rope_cols=d, out_dtype=_f32, seq=s)
            o = _dilated(q.reshape(b, N_CLASSES, per_class, d), kv, batch=b, seq=s, d_model=d)
            w_o = w_o_b[j]
        h = _post_attn(h, o.reshape(b * s, d), w_o, norm_ffn[layer].reshape(1, d), w_gate[layer],
                       w_up[layer], w_down[layer], g_final, final_norm=(layer == depth - 1))
    if kv is not None:
        h = jnp.swapaxes(h.reshape(b, N_CLASSES, per_class, d), 1, 2)
    return h.reshape(b, s, d)
```

```python
import functools

import jax
import jax.numpy as jnp
from jax import lax
from jax.experimental import pallas as pl
from jax.experimental.pallas import tpu as pltpu

HEAD_DIM = 64
BLOCK = 128
ROPE_THETA = 10000.0
NORM_EPS = 1e-6
DILATED_BRANCHES = ((128, 1), (512, 4), (2048, 16))
N_CLASSES = max(r for _, r in DILATED_BRANCHES)

LANES = 128
VMEM_LIMIT_BYTES = 56 * 1024 * 1024
MASK_VALUE = -1e30
EXP2_UNDERFLOW = -160.0
LOG2_E = 1.4426950408889634
NEAR_KEYS = 192
MXU_WIDTH = 256

_f32 = jnp.float32
_bf16 = jnp.bfloat16


def _rms(x, g):
    return x * lax.rsqrt(jnp.mean(x * x, axis=-1, keepdims=True) + NORM_EPS) * g


def _compiler_params(semantics):
    return pltpu.CompilerParams(dimension_semantics=semantics, vmem_limit_bytes=VMEM_LIMIT_BYTES)


def _head_lanes():
    lane = lax.broadcasted_iota(jnp.int32, (1, LANES), 1)
    return lane < HEAD_DIM, lane >= HEAD_DIM


def _by_head(x, head_lanes):
    return jnp.concatenate([jnp.where(hm, x, jnp.zeros_like(x)) for hm in head_lanes], axis=0)


def _norm_proj_kernel(x_ref, g_ref, w_ref, cos_ref, sin_ref, o_ref, *, rope_cols):
    xn = _rms(x_ref[...], g_ref[...]).astype(_bf16)
    n_out = o_ref.shape[1]
    tn = MXU_WIDTH
    if rope_cols:
        reps = tn // LANES
        cos = jnp.concatenate([cos_ref[...]] * reps, axis=1)
        sin = jnp.concatenate([sin_ref[...]] * reps, axis=1)
        first_half = (lax.broadcasted_iota(jnp.int32, cos.shape, 1) % HEAD_DIM) < HEAD_DIM // 2
    for c in range(n_out // tn):
        cols = slice(c * tn, (c + 1) * tn)
        y = jnp.dot(xn, w_ref[:, cols], preferred_element_type=_f32)
        if c * tn < rope_cols:
            partner = jnp.where(first_half, pltpu.roll(y, tn - HEAD_DIM // 2, 1),
                                pltpu.roll(y, HEAD_DIM // 2, 1))
            y = y * cos + partner * sin
        o_ref[:, cols] = y.astype(o_ref.dtype)


def _norm_proj(x, g, w, cos, sin, *, rope_cols, out_dtype, seq, tm=512):
    m, d = x.shape
    n_out = w.shape[1]
    pos_blocks = seq // tm
    return pl.pallas_call(
        functools.partial(_norm_proj_kernel, rope_cols=rope_cols),
        grid=(m // tm,),
        in_specs=[
            pl.BlockSpec((tm, d), lambda i: (i, 0)),
            pl.BlockSpec((1, d), lambda i: (0, 0)),
            pl.BlockSpec((d, n_out), lambda i: (0, 0)),
            pl.BlockSpec((tm, LANES), lambda i: (i % pos_blocks, 0)),
            pl.BlockSpec((tm, LANES), lambda i: (i % pos_blocks, 0)),
        ],
        out_specs=pl.BlockSpec((tm, n_out), lambda i: (i, 0)),
        out_shape=jax.ShapeDtypeStruct((m, n_out), out_dtype),
        compiler_params=_compiler_params(("parallel",)),
        name="norm_proj",
    )(x, g, w, cos, sin)


def _stickbreak_kernel(q_ref, k_ref, v_ref, uu_ref, o_ref, acc_ref, carry_ref, *, tq, tk):
    def tile(qi, carry):
        rows = pl.ds(pl.multiple_of(qi * tq, tq), tq)
        _stickbreak_tile(qi, q_ref.at[rows], k_ref, v_ref, uu_ref, o_ref.at[rows], acc_ref, carry_ref,
                         tq=tq, tk=tk)
        return carry

    lax.fori_loop(0, q_ref.shape[0] // tq, tile, 0)


def _stickbreak_tile(qi, q_ref, k_ref, v_ref, uu_ref, o_ref, acc_ref, carry_ref, *, tq, tk):
    head_lanes = _head_lanes()
    q = q_ref[...] * jnp.asarray(HEAD_DIM ** -0.5, _bf16)
    uu = uu_ref[...]
    strict = (lax.broadcasted_iota(jnp.int32, (tk, tk), 1)
              < lax.broadcasted_iota(jnp.int32, (tk, tk), 0))
    sign_bit = jnp.uint32(0x80000000)

    def mask_top(a, diagonal):
        if not diagonal:
            return a
        top = jnp.where(strict, a[:tk], 0.0)
        return top if a.shape[0] == tk else jnp.concatenate([top, a[tk:]], axis=0)

    def add_rows(x, r0, r1, delta):
        parts = ([x[:r0]] if r0 else []) + [x[r0:r1] + delta] + ([x[r1:]] if r1 < tq else [])
        return parts[0] if len(parts) == 1 else jnp.concatenate(parts, axis=0)

    def key_blocks(blocks, state):
        carries, acc = list(state[:2]), state[2]
        values, logits, afters = [], {}, {}
        for b, (jb, r0, r1, _) in enumerate(blocks):
            start = pl.multiple_of(jb * tk, tk)
            z = lax.dot_general(q[r0:r1], _by_head(k_ref[pl.ds(start, tk), :], head_lanes),
                                (((1,), (1,)), ((), ())), preferred_element_type=_f32)
            logits[b, 0], logits[b, 1] = z[:, :tk] * LOG2_E, z[:, tk:] * LOG2_E
            values.append(_by_head(v_ref[pl.ds(start, tk), :], head_lanes))
        for b, (_, _, _, diagonal) in enumerate(blocks):
            for h in range(2):
                z = logits[b, h]
                neg_abs = lax.bitcast_convert_type(lax.bitcast_convert_type(z, jnp.uint32) | sign_bit, _f32)
                softplus = jnp.maximum(z, 0.0) + jnp.log2(1.0 + jnp.exp2(neg_abs))
                sp = mask_top(softplus, diagonal)
                hi = sp.astype(_bf16)
                lo = (sp - hi.astype(_f32)).astype(_bf16)
                afters[b, h] = jnp.dot(jnp.concatenate([hi, lo], axis=1), uu,
                                       preferred_element_type=_f32)
        for b, (_, r0, r1, diagonal) in enumerate(blocks):
            weights = []
            for h in range(2):
                after = afters[b, h]
                w = mask_top(jnp.exp2(logits[b, h] + after + carries[h][r0:r1]), diagonal)
                weights.append(w.astype(_bf16))
                carries[h] = add_rows(carries[h], r0, r1, jnp.broadcast_to(after[:, :1], after.shape))
            pv = jnp.dot(jnp.concatenate(weights, axis=1), values[b], preferred_element_type=_f32)
            acc = add_rows(acc, r0, r1, pv)
        return carries[0], carries[1], acc

    blocks_per_tile = tq // tk
    first_diag = qi * blocks_per_tile

    def near_keys(with_previous_tile):
        zero = jnp.zeros((tq, LANES), _f32)
        blocks = [(first_diag + d, d * tk, tq, True) for d in reversed(range(blocks_per_tile))]
        if with_previous_tile:
            blocks += [(first_diag - 1 - d, 0, skipped_from[d], False)
                       for d in range(blocks_per_tile) if skipped_from[d]]
        return key_blocks(blocks, (zero, zero, zero))

    skipped_from = [min(tq, max(NEAR_KEYS - d * tk, 0)) for d in range(blocks_per_tile)]
    partly_skipped = [d for d in range(blocks_per_tile) if skipped_from[d] < tq]

    def carry_max(state, r0=0):
        return jnp.max(jnp.maximum(state[0][r0:], state[1][r0:]))

    def save(state):
        carry_ref[0], carry_ref[1], acc_ref[...] = state
        return state

    def load():
        return carry_ref[0], carry_ref[1], acc_ref[...]

    def first_tile():
        c_max = carry_max(save(near_keys(False)))
        return (c_max,) * (1 + len(partly_skipped))

    def later_tile():
        state = save(near_keys(True))
        return (carry_max(state),) + tuple(carry_max(state, skipped_from[d]) for d in partly_skipped)

    c_max, *c_max_skipped = lax.cond(qi > 0, later_tile, first_tile)

    for d, c_max_rows in zip(partly_skipped, c_max_skipped):
        @pl.when((qi > 0) & (c_max_rows > EXP2_UNDERFLOW))
        def _(d=d):
            save(key_blocks([(first_diag - 1 - d, skipped_from[d], tq, False)], load()))

    def more(loop):
        t, c_max = loop
        return (t < qi) & (c_max > EXP2_UNDERFLOW)

    def older_keys(loop):
        t, _ = loop
        blocks = [(first_diag - 1 - t * blocks_per_tile - d, 0, tq, False) for d in range(blocks_per_tile)]
        return t + 1, carry_max(save(key_blocks(blocks, load())))

    lax.while_loop(more, older_keys, (jnp.int32(1), c_max))
    o_ref[...] = acc_ref[...].astype(o_ref.dtype)


def _cumsum_matrix(tk):
    src = jnp.arange(2 * tk)[:, None] % tk
    dst = jnp.arange(tk)[None, :]
    return -(src >= dst).astype(_bf16)


def _stickbreak(qkv, *, batch, seq, d_model, tq=256, tk=128):
    pairs = d_model // LANES
    return pl.pallas_call(
        functools.partial(_stickbreak_kernel, tq=tq, tk=tk),
        grid=(batch, pairs),
        in_specs=[
            pl.BlockSpec((None, seq, LANES), lambda b, p: (b, 0, p)),
            pl.BlockSpec((None, seq, LANES), lambda b, p: (b, 0, pairs + p)),
            pl.BlockSpec((None, seq, LANES), lambda b, p: (b, 0, 2 * pairs + p)),
            pl.BlockSpec((2 * tk, tk), lambda b, p: (0, 0)),
        ],
        out_specs=pl.BlockSpec((None, seq, LANES), lambda b, p: (b, 0, p)),
        out_shape=jax.ShapeDtypeStruct((batch, seq, d_model), _bf16),
        scratch_shapes=[pltpu.VMEM((tq, LANES), _f32), pltpu.VMEM((2, tq, tk), _f32)],
        compiler_params=_compiler_params(("parallel", "parallel")),
        name="stickbreak",
    )(qkv, qkv, qkv, _cumsum_matrix(tk))


def _dilated_kernel(q_ref, k_ref, v_ref, o_ref, m_ref, l_ref, acc_ref, *, group):
    per_class = q_ref.shape[1]
    tile_start = pl.program_id(2) * per_class
    head_lanes = _head_lanes()
    scale = jnp.asarray(HEAD_DIM ** -0.5, _bf16)
    row = lax.broadcasted_iota(jnp.int32, (BLOCK, 2 * BLOCK), 0)
    col = lax.broadcasted_iota(jnp.int32, (BLOCK, 2 * BLOCK), 1)
    in_prev = col < BLOCK
    ones_by_head = ((lax.broadcasted_iota(jnp.int32, (4 * BLOCK, LANES), 0) < 2 * BLOCK)
                    == (lax.broadcasted_iota(jnp.int32, (4 * BLOCK, LANES), 1) < HEAD_DIM)
                    ).astype(_f32).astype(_bf16)

    for branch, (window, r) in enumerate(DILATED_BRANCHES):
        assert window // r == BLOCK
        n_sub = N_CLASSES // r
        chunk = BLOCK // n_sub
        blocks_per_class = per_class // chunk
        sub_index = lambda a: n_sub * (a % chunk) + a // chunk
        q_sub, k_sub = sub_index(row), sub_index(col % BLOCK)
        band_prev = in_prev & (k_sub >= q_sub)
        band_cur = (~in_prev) & (k_sub <= q_sub)

        def block_group(g, carry, r=r, branch=branch, n_sub=n_sub, chunk=chunk,
                        blocks_per_class=blocks_per_class, band_prev=band_prev, band_cur=band_cur):
            places, vs, valids, logits, probs, maxes = [], [], [], {}, {}, {}
            for u in range(group):
                idx = g * group + u
                c_r = idx // blocks_per_class
                local = pl.multiple_of((idx % blocks_per_class) * chunk, chunk)
                cur = tile_start + local
                has_prev = cur >= chunk
                prev = jnp.maximum(cur - chunk, 0)
                classes = [c_r + r * m for m in range(n_sub)]

                def rows(ref, start):
                    return jnp.concatenate([ref[c, pl.ds(start, chunk), :] for c in classes], axis=0)

                q = rows(q_ref, local).astype(_bf16) * scale
                k = jnp.concatenate([rows(k_ref, prev), rows(k_ref, cur)], axis=0).astype(_bf16)
                v = jnp.concatenate([rows(v_ref, prev), rows(v_ref, cur)], axis=0).astype(_bf16)
                places.append((classes, local))
                vs.append(jnp.concatenate([_by_head(v, head_lanes), ones_by_head], axis=1))
                valids.append((band_prev & has_prev) | band_cur)
                for h in range(2):
                    q_head = jnp.where(head_lanes[h], q, jnp.zeros_like(q))
                    logits[u, h] = lax.dot_general(q_head, k, (((1,), (1,)), ((), ())),
                                                   preferred_element_type=_f32)
            for u in range(group):
                for h in range(2):
                    z = jnp.where(valids[u], logits[u, h], MASK_VALUE)
                    maxes[u, h] = jnp.max(z, axis=1, keepdims=True)
                    probs[u, h] = jnp.exp(z - maxes[u, h]).astype(_bf16)
            for u in range(group):
                classes, local = places[u]
                pv_l = jnp.dot(jnp.concatenate([probs[u, 0], probs[u, 1]], axis=1), vs[u],
                               preferred_element_type=_f32)
                pv_t, l_t = pv_l[:, :LANES], pv_l[:, LANES:]
                m_t = jnp.where(head_lanes[0], maxes[u, 0], maxes[u, 1])
                for m, c in enumerate(classes):
                    part = slice(m * chunk, (m + 1) * chunk)
                    place = (c, pl.ds(local, chunk), slice(None))
                    if branch == 0:
                        m_ref[place] = m_t[part]
                        l_ref[place] = l_t[part]
                        acc_ref[place] = pv_t[part]
                    else:
                        m_old = m_ref[place]
                        m_new = jnp.maximum(m_old, m_t[part])
                        a_old = jnp.exp(m_old - m_new)
                        a_new = jnp.exp(m_t[part] - m_new)
                        m_ref[place] = m_new
                        l_ref[place] = a_old * l_ref[place] + a_new * l_t[part]
                        acc_ref[place] = a_old * acc_ref[place] + a_new * pv_t[part]
            return carry

        lax.fori_loop(0, r * blocks_per_class // group, block_group, 0)

    o_ref[...] = (acc_ref[...] / l_ref[...]).astype(o_ref.dtype)


def _dilated(q, kv, *, batch, seq, d_model, group=4):
    pairs = d_model // LANES
    per_class = seq // N_CLASSES
    assert per_class % BLOCK == 0 and N_CLASSES % group == 0
    return pl.pallas_call(
        functools.partial(_dilated_kernel, group=group),
        grid=(batch, pairs, per_class // BLOCK),
        in_specs=[
            pl.BlockSpec((None, N_CLASSES, BLOCK, LANES), lambda b, p, i: (b, 0, i, p)),
            pl.BlockSpec((None, N_CLASSES, per_class, LANES), lambda b, p, i: (b, 0, 0, p)),
            pl.BlockSpec((None, N_CLASSES, per_class, LANES), lambda b, p, i: (b, 0, 0, pairs + p)),
        ],
        out_specs=pl.BlockSpec((None, N_CLASSES, BLOCK, LANES), lambda b, p, i: (b, 0, i, p)),
        out_shape=jax.ShapeDtypeStruct((batch, N_CLASSES, per_class, d_model), _bf16),
        scratch_shapes=[pltpu.VMEM((N_CLASSES, BLOCK, LANES), _f32)] * 3,
        compiler_params=_compiler_params(("parallel", "parallel", "arbitrary")),
        name="dilated",
    )(q, kv, kv)


def _post_attn_kernel(h_ref, o_ref, wo_ref, g_ref, wg_ref, wu_ref, wd_ref, gf_ref, out_ref, acc_ref,
                      *, tf, final_norm):
    h1 = h_ref[...] + jnp.dot(o_ref[...], wo_ref[...], preferred_element_type=_f32)
    xn = _rms(h1, g_ref[...]).astype(_bf16)
    acc_ref[...] = h1
    for c in range(wg_ref.shape[1] // tf):
        cols = slice(c * tf, (c + 1) * tf)
        gate = jnp.dot(xn, wg_ref[:, cols], preferred_element_type=_f32)
        up = jnp.dot(xn, wu_ref[:, cols], preferred_element_type=_f32)
        act = (gate / (1.0 + jnp.exp(-gate)) * up).astype(_bf16)
        acc_ref[...] += jnp.dot(act, wd_ref[cols, :], preferred_element_type=_f32)
    out = acc_ref[...]
    if final_norm:
        out = _rms(out, gf_ref[...])
    out_ref[...] = out


def _post_attn(h, o, wo, g, wg, wu, wd, g_final, *, final_norm, tm=512, tf=256):
    m, d = h.shape
    f = wg.shape[1]
    resident = lambda shape: pl.BlockSpec(shape, lambda i: (0, 0), pipeline_mode=pl.Buffered(1))
    return pl.pallas_call(
        functools.partial(_post_attn_kernel, tf=tf, final_norm=final_norm),
        grid=(m // tm,),
        in_specs=[
            pl.BlockSpec((tm, d), lambda i: (i, 0)),
            pl.BlockSpec((tm, d), lambda i: (i, 0)),
            resident((d, d)),
            resident((1, d)),
            resident((d, f)),
            resident((d, f)),
            resident((f, d)),
            resident((1, d)),
        ],
        out_specs=pl.BlockSpec((tm, d), lambda i: (i, 0)),
        out_shape=jax.ShapeDtypeStruct((m, d), _f32),
        scratch_shapes=[pltpu.VMEM((tm, d), _f32)],
        compiler_params=_compiler_params(("parallel",)),
        name="post_attn",
    )(h, o, wo, g, wg, wu, wd, g_final)


def _rope_tables(seq):
    inv_freq = ROPE_THETA ** (-jnp.arange(0, HEAD_DIM, 2, dtype=_f32) / HEAD_DIM)
    ang = jnp.arange(seq, dtype=_f32)[:, None] * inv_freq[None, :]
    cos, sin = jnp.cos(ang), jnp.sin(ang)
    reps = LANES // HEAD_DIM
    return (jnp.tile(jnp.concatenate([cos, cos], axis=1), (1, reps)),
            jnp.tile(jnp.concatenate([-sin, sin], axis=1), (1, reps)))


def _class_major(a, lead):
    s, f = a.shape[-2:]
    return jnp.swapaxes(a.reshape(*lead, s // N_CLASSES, N_CLASSES, f), -3, -2)


def kernel(x, norm_mix, w_qkv_a, w_o_a, norm_kv, w_kv, w_q_b, w_o_b, norm_ffn, w_gate, w_up, w_down,
           norm_final):
    b, s, d = x.shape
    depth = norm_mix.shape[0]
    n_a = w_qkv_a.shape[0]
    assert d % LANES == 0 and s % (N_CLASSES * BLOCK) == 0
    cos, sin = _rope_tables(s)
    cos_cm, sin_cm = (_class_major(t, ()).reshape(s, LANES) for t in (cos, sin))
    cast = lambda w: w.astype(_bf16)
    w_qkv_a, w_o_a, w_kv, w_q_b, w_o_b = map(cast, (w_qkv_a, w_o_a, w_kv, w_q_b, w_o_b))
    w_gate, w_up, w_down = map(cast, (w_gate, w_up, w_down))
    g_final = norm_final.reshape(1, d)
    per_class = s // N_CLASSES

    h = x.reshape(b * s, d)
    kv = None
    for layer in range(depth):
        g_mix = norm_mix[layer].reshape(1, d)
        if layer < n_a:
            qkv = _norm_proj(h, g_mix, w_qkv_a[layer], cos, sin, rope_cols=0, out_dtype=_bf16, seq=s)
            o = _stickbreak(qkv.reshape(b, s, 3 * d), batch=b, seq=s, d_model=d)
            w_o = w_o_a[layer]
        else:
            j = layer - n_a
            if kv is None:
                h = _class_major(h.reshape(b, s, d), (b,)).reshape(b * s, d)
                kv = _norm_proj(h, norm_kv.reshape(1, d), w_kv, cos_cm, sin_cm, rope_cols=d, out_dtype=_f32,
                                seq=s).reshape(b, N_CLASSES, per_class, 2 * d)
            q = _norm_proj(h, g_mix, w_q_b[j], cos_cm, sin_cm, rope_cols=d, out_dtype=_f32, seq=s)
            o = _dilated(q.reshape(b, N_CLASSES, per_class, d), kv, batch=b, seq=s, d_model=d)
            w_o = w_o_b[j]
        h = _post_attn(h, o.reshape(b * s, d), w_o, norm_ffn[layer].reshape(1, d), w_gate[layer],
                       w_up[layer], w_down[layer], g_final, final_norm=(layer == depth - 1))
    if kv is not None:
        h = jnp.swapaxes(h.reshape(b, N_CLASSES, per_class, d), 1, 2)
    return h.reshape(b, s, d)
```

```python
import functools

import jax
import jax.numpy as jnp
from jax import lax
from jax.experimental import pallas as pl
from jax.experimental.pallas import tpu as pltpu

HEAD_DIM = 64
BLOCK = 128
ROPE_THETA = 10000.0
NORM_EPS = 1e-6
DILATED_BRANCHES = ((128, 1), (512, 4), (2048, 16))
N_CLASSES = max(r for _, r in DILATED_BRANCHES)

LANES = 128
VMEM_LIMIT_BYTES = 56 * 1024 * 1024
MASK_VALUE = -1e30
EXP2_UNDERFLOW = -160.0
LOG2_E = 1.4426950408889634
NEAR_KEYS = 192
MXU_WIDTH = 256

_f32 = jnp.float32
_bf16 = jnp.bfloat16


def _rms(x, g):
    return x * lax.rsqrt(jnp.mean(x * x, axis=-1, keepdims=True) + NORM_EPS) * g


def _compiler_params(semantics):
    return pltpu.CompilerParams(dimension_semantics=semantics, vmem_limit_bytes=VMEM_LIMIT_BYTES)


def _head_lanes():
    lane = lax.broadcasted_iota(jnp.int32, (1, LANES), 1)
    return lane < HEAD_DIM, lane >= HEAD_DIM


def _by_head(x, head_lanes):
    return jnp.concatenate([jnp.where(hm, x, jnp.zeros_like(x)) for hm in head_lanes], axis=0)


def _norm_proj_kernel(x_ref, g_ref, w_ref, cos_ref, sin_ref, o_ref, *, rope_cols):
    xn = _rms(x_ref[...], g_ref[...]).astype(_bf16)
    n_out = o_ref.shape[1]
    tn = MXU_WIDTH
    if rope_cols:
        reps = tn // LANES
        cos = jnp.concatenate([cos_ref[...]] * reps, axis=1)
        sin = jnp.concatenate([sin_ref[...]] * reps, axis=1)
        first_half = (lax.broadcasted_iota(jnp.int32, cos.shape, 1) % HEAD_DIM) < HEAD_DIM // 2
    for c in range(n_out // tn):
        cols = slice(c * tn, (c + 1) * tn)
        y = jnp.dot(xn, w_ref[:, cols], preferred_element_type=_f32)
        if c * tn < rope_cols:
            partner = jnp.where(first_half, pltpu.roll(y, tn - HEAD_DIM // 2, 1),
                                pltpu.roll(y, HEAD_DIM // 2, 1))
            y = y * cos + partner * sin
        o_ref[:, cols] = y.astype(o_ref.dtype)


def _norm_proj(x, g, w, cos, sin, *, rope_cols, out_dtype, seq, tm=512):
    m, d = x.shape
    n_out = w.shape[1]
    pos_blocks = seq // tm
    return pl.pallas_call(
        functools.partial(_norm_proj_kernel, rope_cols=rope_cols),
        grid=(m // tm,),
        in_specs=[
            pl.BlockSpec((tm, d), lambda i: (i, 0)),
            pl.BlockSpec((1, d), lambda i: (0, 0)),
            pl.BlockSpec((d, n_out), lambda i: (0, 0)),
            pl.BlockSpec((tm, LANES), lambda i: (i % pos_blocks, 0)),
            pl.BlockSpec((tm, LANES), lambda i: (i % pos_blocks, 0)),
        ],
        out_specs=pl.BlockSpec((tm, n_out), lambda i: (i, 0)),
        out_shape=jax.ShapeDtypeStruct((m, n_out), out_dtype),
        compiler_params=_compiler_params(("parallel",)),
        name="norm_proj",
    )(x, g, w, cos, sin)


def _stickbreak_kernel(q_ref, k_ref, v_ref, uu_ref, o_ref, acc_ref, carry_ref, *, tq, tk):
    def tile(qi, carry, first_tile=False):
        rows = pl.ds(pl.multiple_of(qi * tq, tq), tq)
        _stickbreak_tile(qi, q_ref.at[rows], k_ref, v_ref, uu_ref, o_ref.at[rows], acc_ref, carry_ref,
                         tq=tq, tk=tk, first_tile=first_tile)
        return carry

    tile(0, 0, first_tile=True)
    lax.fori_loop(1, q_ref.shape[0] // tq, tile, 0)


def _stickbreak_tile(qi, q_ref, k_ref, v_ref, uu_ref, o_ref, acc_ref, carry_ref, *, tq, tk, first_tile):
    head_lanes = _head_lanes()
    q = q_ref[...] * jnp.asarray(HEAD_DIM ** -0.5, _bf16)
    uu = uu_ref[...]
    strict = (lax.broadcasted_iota(jnp.int32, (tk, tk), 1)
              < lax.broadcasted_iota(jnp.int32, (tk, tk), 0))
    sign_bit = jnp.uint32(0x80000000)

    def mask_top(a, diagonal):
        if not diagonal:
            return a
        top = jnp.where(strict, a[:tk], 0.0)
        return top if a.shape[0] == tk else jnp.concatenate([top, a[tk:]], axis=0)

    def add_rows(x, r0, r1, delta):
        parts = ([x[:r0]] if r0 else []) + [x[r0:r1] + delta] + ([x[r1:]] if r1 < tq else [])
        return parts[0] if len(parts) == 1 else jnp.concatenate(parts, axis=0)

    def key_blocks(blocks, state):
        carries, acc = list(state[:2]), state[2]
        values, logits, afters = [], {}, {}
        for b, (jb, r0, r1, _) in enumerate(blocks):
            start = pl.multiple_of(jb * tk, tk)
            z = lax.dot_general(q[r0:r1], _by_head(k_ref[pl.ds(start, tk), :], head_lanes),
                                (((1,), (1,)), ((), ())), preferred_element_type=_f32)
            logits[b, 0], logits[b, 1] = z[:, :tk] * LOG2_E, z[:, tk:] * LOG2_E
            values.append(_by_head(v_ref[pl.ds(start, tk), :], head_lanes))
        for b, (_, _, _, diagonal) in enumerate(blocks):
            for h in range(2):
                z = logits[b, h]
                neg_abs = lax.bitcast_convert_type(lax.bitcast_convert_type(z, jnp.uint32) | sign_bit, _f32)
                softplus = jnp.maximum(z, 0.0) + jnp.log2(1.0 + jnp.exp2(neg_abs))
                sp = mask_top(softplus, diagonal)
                hi = sp.astype(_bf16)
                lo = (sp - hi.astype(_f32)).astype(_bf16)
                afters[b, h] = jnp.dot(jnp.concatenate([hi, lo], axis=1), uu,
                                       preferred_element_type=_f32)
        for b, (_, r0, r1, diagonal) in enumerate(blocks):
            weights = []
            for h in range(2):
                after = afters[b, h]
                w = mask_top(jnp.exp2(logits[b, h] + after + carries[h][r0:r1]), diagonal)
                weights.append(w.astype(_bf16))
                carries[h] = add_rows(carries[h], r0, r1, jnp.broadcast_to(after[:, :1], after.shape))
            pv = jnp.dot(jnp.concatenate(weights, axis=1), values[b], preferred_element_type=_f32)
            acc = add_rows(acc, r0, r1, pv)
        return carries[0], carries[1], acc

    blocks_per_tile = tq // tk
    first_diag = qi * blocks_per_tile

    def near_keys(with_previous_tile):
        zero = jnp.zeros((tq, LANES), _f32)
        blocks = [(first_diag + d, d * tk, tq, True) for d in reversed(range(blocks_per_tile))]
        if with_previous_tile:
            blocks += [(first_diag - 1 - d, 0, skipped_from[d], False)
                       for d in range(blocks_per_tile) if skipped_from[d]]
        return key_blocks(blocks, (zero, zero, zero))

    skipped_from = [min(tq, max(NEAR_KEYS - d * tk, 0)) for d in range(blocks_per_tile)]
    partly_skipped = [d for d in range(blocks_per_tile) if skipped_from[d] < tq]

    def carry_max(state, r0=0):
        return jnp.max(jnp.maximum(state[0][r0:], state[1][r0:]))

    def save(state):
        carry_ref[0], carry_ref[1], acc_ref[...] = state
        return state

    def load():
        return carry_ref[0], carry_ref[1], acc_ref[...]

    state = near_keys(not first_tile)
    o_ref[...] = state[2].astype(o_ref.dtype)
    if first_tile:
        return
    c_max = carry_max(state)

    def more(loop):
        t, c_max = loop
        return (t < qi) & (c_max > EXP2_UNDERFLOW)

    def older_keys(loop):
        t, _ = loop
        blocks = [(first_diag - 1 - t * blocks_per_tile - d, 0, tq, False) for d in range(blocks_per_tile)]
        return t + 1, carry_max(save(key_blocks(blocks, load())))

    @pl.when(c_max > EXP2_UNDERFLOW)
    def _():
        save(state)
        for d in partly_skipped:
            save(key_blocks([(first_diag - 1 - d, skipped_from[d], tq, False)], load()))
        lax.while_loop(more, older_keys, (jnp.int32(1), c_max))
        o_ref[...] = acc_ref[...].astype(o_ref.dtype)


def _cumsum_matrix(tk):
    src = jnp.arange(2 * tk)[:, None] % tk
    dst = jnp.arange(tk)[None, :]
    return -(src >= dst).astype(_bf16)


def _stickbreak(qkv, *, batch, seq, d_model, tq=256, tk=128):
    pairs = d_model // LANES
    return pl.pallas_call(
        functools.partial(_stickbreak_kernel, tq=tq, tk=tk),
        grid=(batch, pairs),
        in_specs=[
            pl.BlockSpec((None, seq, LANES), lambda b, p: (b, 0, p)),
            pl.BlockSpec((None, seq, LANES), lambda b, p: (b, 0, pairs + p)),
            pl.BlockSpec((None, seq, LANES), lambda b, p: (b, 0, 2 * pairs + p)),
            pl.BlockSpec((2 * tk, tk), lambda b, p: (0, 0)),
        ],
        out_specs=pl.BlockSpec((None, seq, LANES), lambda b, p: (b, 0, p)),
        out_shape=jax.ShapeDtypeStruct((batch, seq, d_model), _bf16),
        scratch_shapes=[pltpu.VMEM((tq, LANES), _f32), pltpu.VMEM((2, tq, tk), _f32)],
        compiler_params=_compiler_params(("parallel", "parallel")),
        name="stickbreak",
    )(qkv, qkv, qkv, _cumsum_matrix(tk))


def _dilated_kernel(q_ref, k_ref, v_ref, o_ref, m_ref, l_ref, acc_ref, *, group):
    per_class = q_ref.shape[1]
    tile_start = pl.program_id(2) * per_class
    head_lanes = _head_lanes()
    scale = jnp.asarray(HEAD_DIM ** -0.5, _bf16)
    row = lax.broadcasted_iota(jnp.int32, (BLOCK, 2 * BLOCK), 0)
    col = lax.broadcasted_iota(jnp.int32, (BLOCK, 2 * BLOCK), 1)
    in_prev = col < BLOCK
    ones_by_head = ((lax.broadcasted_iota(jnp.int32, (4 * BLOCK, LANES), 0) < 2 * BLOCK)
                    == (lax.broadcasted_iota(jnp.int32, (4 * BLOCK, LANES), 1) < HEAD_DIM)
                    ).astype(_f32).astype(_bf16)

    for branch, (window, r) in enumerate(DILATED_BRANCHES):
        assert window // r == BLOCK
        n_sub = N_CLASSES // r
        chunk = BLOCK // n_sub
        blocks_per_class = per_class // chunk
        sub_index = lambda a: n_sub * (a % chunk) + a // chunk
        q_sub, k_sub = sub_index(row), sub_index(col % BLOCK)
        band_prev = in_prev & (k_sub >= q_sub)
        band_cur = (~in_prev) & (k_sub <= q_sub)

        def block_group(g, carry, r=r, branch=branch, n_sub=n_sub, chunk=chunk,
                        blocks_per_class=blocks_per_class, band_prev=band_prev, band_cur=band_cur):
            places, vs, valids, logits, probs, maxes = [], [], [], {}, {}, {}
            for u in range(group):
                idx = g * group + u
                c_r = idx // blocks_per_class
                local = pl.multiple_of((idx % blocks_per_class) * chunk, chunk)
                cur = tile_start + local
                has_prev = cur >= chunk
                prev = jnp.maximum(cur - chunk, 0)
                classes = [c_r + r * m for m in range(n_sub)]

                def rows(ref, start):
                    return jnp.concatenate([ref[c, pl.ds(start, chunk), :] for c in classes], axis=0)

                q = rows(q_ref, local).astype(_bf16) * scale
                k = jnp.concatenate([rows(k_ref, prev), rows(k_ref, cur)], axis=0).astype(_bf16)
                v = jnp.concatenate([rows(v_ref, prev), rows(v_ref, cur)], axis=0).astype(_bf16)
                places.append((classes, local))
                vs.append(jnp.concatenate([_by_head(v, head_lanes), ones_by_head], axis=1))
                valids.append((band_prev & has_prev) | band_cur)
                for h in range(2):
                    q_head = jnp.where(head_lanes[h], q, jnp.zeros_like(q))
                    logits[u, h] = lax.dot_general(q_head, k, (((1,), (1,)), ((), ())),
                                                   preferred_element_type=_f32)
            for u in range(group):
                for h in range(2):
                    z = jnp.where(valids[u], logits[u, h], MASK_VALUE)
                    maxes[u, h] = jnp.max(z, axis=1, keepdims=True)
                    probs[u, h] = jnp.exp(z - maxes[u, h]).astype(_bf16)
            for u in range(group):
                classes, local = places[u]
                pv_l = jnp.dot(jnp.concatenate([probs[u, 0], probs[u, 1]], axis=1), vs[u],
                               preferred_element_type=_f32)
                pv_t, l_t = pv_l[:, :LANES], pv_l[:, LANES:]
                m_t = jnp.where(head_lanes[0], maxes[u, 0], maxes[u, 1])
                for m, c in enumerate(classes):
                    part = slice(m * chunk, (m + 1) * chunk)
                    place = (c, pl.ds(local, chunk), slice(None))
                    if branch == 0:
                        m_ref[place] = m_t[part]
                        l_ref[place] = l_t[part]
                        acc_ref[place] = pv_t[part]
                    else:
                        m_old = m_ref[place]
                        m_new = jnp.maximum(m_old, m_t[part])
                        a_old = jnp.exp(m_old - m_new)
                        a_new = jnp.exp(m_t[part] - m_new)
                        m_ref[place] = m_new
                        l_ref[place] = a_old * l_ref[place] + a_new * l_t[part]
                        acc_ref[place] = a_old * acc_ref[place] + a_new * pv_t[part]
            return carry

        lax.fori_loop(0, r * blocks_per_class // group, block_group, 0)

    o_ref[...] = (acc_ref[...] / l_ref[...]).astype(o_ref.dtype)


def _dilated(q, kv, *, batch, seq, d_model, group=4):
    pairs = d_model // LANES
    per_class = seq // N_CLASSES
    assert per_class % BLOCK == 0 and N_CLASSES % group == 0
    return pl.pallas_call(
        functools.partial(_dilated_kernel, group=group),
        grid=(batch, pairs, per_class // BLOCK),
        in_specs=[
            pl.BlockSpec((None, N_CLASSES, BLOCK, LANES), lambda b, p, i: (b, 0, i, p)),
            pl.BlockSpec((None, N_CLASSES, per_class, LANES), lambda b, p, i: (b, 0, 0, p)),
            pl.BlockSpec((None, N_CLASSES, per_class, LANES), lambda b, p, i: (b, 0, 0, pairs + p)),
        ],
        out_specs=pl.BlockSpec((None, N_CLASSES, BLOCK, LANES), lambda b, p, i: (b, 0, i, p)),
        out_shape=jax.ShapeDtypeStruct((batch, N_CLASSES, per_class, d_model), _bf16),
        scratch_shapes=[pltpu.VMEM((N_CLASSES, BLOCK, LANES), _f32)] * 3,
        compiler_params=_compiler_params(("parallel", "parallel", "arbitrary")),
        name="dilated",
    )(q, kv, kv)


def _post_attn_kernel(h_ref, o_ref, wo_ref, g_ref, wg_ref, wu_ref, wd_ref, gf_ref, out_ref, acc_ref,
                      *, tf, final_norm):
    h1 = h_ref[...] + jnp.dot(o_ref[...], wo_ref[...], preferred_element_type=_f32)
    xn = _rms(h1, g_ref[...]).astype(_bf16)
    acc_ref[...] = h1
    for c in range(wg_ref.shape[1] // tf):
        cols = slice(c * tf, (c + 1) * tf)
        gate = jnp.dot(xn, wg_ref[:, cols], preferred_element_type=_f32)
        up = jnp.dot(xn, wu_ref[:, cols], preferred_element_type=_f32)
        act = (gate / (1.0 + jnp.exp(-gate)) * up).astype(_bf16)
        acc_ref[...] += jnp.dot(act, wd_ref[cols, :], preferred_element_type=_f32)
    out = acc_ref[...]
    if final_norm:
        out = _rms(out, gf_ref[...])
    out_ref[...] = out


def _post_attn(h, o, wo, g, wg, wu, wd, g_final, *, final_norm, tm=512, tf=256):
    m, d = h.shape
    f = wg.shape[1]
    resident = lambda shape: pl.BlockSpec(shape, lambda i: (0, 0), pipeline_mode=pl.Buffered(1))
    return pl.pallas_call(
        functools.partial(_post_attn_kernel, tf=tf, final_norm=final_norm),
        grid=(m // tm,),
        in_specs=[
            pl.BlockSpec((tm, d), lambda i: (i, 0)),
            pl.BlockSpec((tm, d), lambda i: (i, 0)),
            resident((d, d)),
            resident((1, d)),
            resident((d, f)),
            resident((d, f)),
            resident((f, d)),
            resident((1, d)),
        ],
        out_specs=pl.BlockSpec((tm, d), lambda i: (i, 0)),
        out_shape=jax.ShapeDtypeStruct((m, d), _f32),
        scratch_shapes=[pltpu.VMEM((tm, d), _f32)],
        compiler_params=_compiler_params(("parallel",)),
        name="post_attn",
    )(h, o, wo, g, wg, wu, wd, g_final)


def _rope_tables(seq):
    inv_freq = ROPE_THETA ** (-jnp.arange(0, HEAD_DIM, 2, dtype=_f32) / HEAD_DIM)
    ang = jnp.arange(seq, dtype=_f32)[:, None] * inv_freq[None, :]
    cos, sin = jnp.cos(ang), jnp.sin(ang)
    reps = LANES // HEAD_DIM
    return (jnp.tile(jnp.concatenate([cos, cos], axis=1), (1, reps)),
            jnp.tile(jnp.concatenate([-sin, sin], axis=1), (1, reps)))


def _class_major(a, lead):
    s, f = a.shape[-2:]
    return jnp.swapaxes(a.reshape(*lead, s // N_CLASSES, N_CLASSES, f), -3, -2)


def kernel(x, norm_mix, w_qkv_a, w_o_a, norm_kv, w_kv, w_q_b, w_o_b, norm_ffn, w_gate, w_up, w_down,
           norm_final):
    b, s, d = x.shape
    depth = norm_mix.shape[0]
    n_a = w_qkv_a.shape[0]
    assert d % LANES == 0 and s % (N_CLASSES * BLOCK) == 0
    cos, sin = _rope_tables(s)
    cos_cm, sin_cm = (_class_major(t, ()).reshape(s, LANES) for t in (cos, sin))
    cast = lambda w: w.astype(_bf16)
    w_qkv_a, w_o_a, w_kv, w_q_b, w_o_b = map(cast, (w_qkv_a, w_o_a, w_kv, w_q_b, w_o_b))
    w_gate, w_up, w_down = map(cast, (w_gate, w_up, w_down))
    g_final = norm_final.reshape(1, d)
    per_class = s // N_CLASSES

    h = x.reshape(b * s, d)
    kv = None
    for layer in range(depth):
        g_mix = norm_mix[layer].reshape(1, d)
        if layer < n_a:
            qkv = _norm_proj(h, g_mix, w_qkv_a[layer], cos, sin, rope_cols=0, out_dtype=_bf16, seq=s)
            o = _stickbreak(qkv.reshape(b, s, 3 * d), batch=b, seq=s, d_model=d)
            w_o = w_o_a[layer]
        else:
            j = layer - n_a
            if kv is None:
                h = _class_major(h.reshape(b, s, d), (b,)).reshape(b * s, d)
                kv = _norm_proj(h, norm_kv.reshape(1, d), w_kv, cos_cm, sin_cm, rope_cols=d, out_dtype=_f32,
                                seq=s).reshape(b, N_CLASSES, per_class, 2 * d)
            q = _norm_proj(h, g_mix, w_q_b[j], cos_cm, sin_cm, rope_cols=d, out_dtype=_f32, seq=s)
            o = _dilated(q.reshape(b, N_CLASSES, per_class, d), kv, batch=b, seq=s, d_model=d)
            w_o = w_o_b[j]
        h = _post_attn(h, o.reshape(b * s, d), w_o, norm_ffn[layer].reshape(1, d), w_gate[layer],
                       w_up[layer], w_down[layer], g_final, final_norm=(layer == depth - 1))
    if kv is not None:
        h = jnp.swapaxes(h.reshape(b, N_CLASSES, per_class, d), 1, 2)
    return h.reshape(b, s, d)
```

```python
import functools

import jax
import jax.numpy as jnp
from jax import lax
from jax.experimental import pallas as pl
from jax.experimental.pallas import tpu as pltpu

HEAD_DIM = 64
BLOCK = 128
ROPE_THETA = 10000.0
NORM_EPS = 1e-6
DILATED_BRANCHES = ((128, 1), (512, 4), (2048, 16))
N_CLASSES = max(r for _, r in DILATED_BRANCHES)

LANES = 128
VMEM_LIMIT_BYTES = 56 * 1024 * 1024
MASK_VALUE = -1e30
EXP2_UNDERFLOW = -160.0
LOG2_E = 1.4426950408889634
NEAR_KEYS = 176
MXU_WIDTH = 256

_f32 = jnp.float32
_bf16 = jnp.bfloat16


def _rms(x, g):
    return x * lax.rsqrt(jnp.mean(x * x, axis=-1, keepdims=True) + NORM_EPS) * g


def _compiler_params(semantics):
    return pltpu.CompilerParams(dimension_semantics=semantics, vmem_limit_bytes=VMEM_LIMIT_BYTES)


def _head_lanes():
    lane = lax.broadcasted_iota(jnp.int32, (1, LANES), 1)
    return lane < HEAD_DIM, lane >= HEAD_DIM


def _by_head(x, head_lanes):
    return jnp.concatenate([jnp.where(hm, x, jnp.zeros_like(x)) for hm in head_lanes], axis=0)


def _norm_proj_kernel(x_ref, g_ref, w_ref, cos_ref, sin_ref, o_ref, *, rope_cols):
    xn = _rms(x_ref[...], g_ref[...]).astype(_bf16)
    n_out = o_ref.shape[1]
    tn = MXU_WIDTH
    if rope_cols:
        reps = tn // LANES
        cos = jnp.concatenate([cos_ref[...]] * reps, axis=1)
        sin = jnp.concatenate([sin_ref[...]] * reps, axis=1)
        first_half = (lax.broadcasted_iota(jnp.int32, cos.shape, 1) % HEAD_DIM) < HEAD_DIM // 2
    for c in range(n_out // tn):
        cols = slice(c * tn, (c + 1) * tn)
        y = jnp.dot(xn, w_ref[:, cols], preferred_element_type=_f32)
        if c * tn < rope_cols:
            partner = jnp.where(first_half, pltpu.roll(y, tn - HEAD_DIM // 2, 1),
                                pltpu.roll(y, HEAD_DIM // 2, 1))
            y = y * cos + partner * sin
        o_ref[:, cols] = y.astype(o_ref.dtype)


def _norm_proj(x, g, w, cos, sin, *, rope_cols, out_dtype, seq, tm=512):
    m, d = x.shape
    n_out = w.shape[1]
    pos_blocks = seq // tm
    return pl.pallas_call(
        functools.partial(_norm_proj_kernel, rope_cols=rope_cols),
        grid=(m // tm,),
        in_specs=[
            pl.BlockSpec((tm, d), lambda i: (i, 0)),
            pl.BlockSpec((1, d), lambda i: (0, 0)),
            pl.BlockSpec((d, n_out), lambda i: (0, 0)),
            pl.BlockSpec((tm, LANES), lambda i: (i % pos_blocks, 0)),
            pl.BlockSpec((tm, LANES), lambda i: (i % pos_blocks, 0)),
        ],
        out_specs=pl.BlockSpec((tm, n_out), lambda i: (i, 0)),
        out_shape=jax.ShapeDtypeStruct((m, n_out), out_dtype),
        compiler_params=_compiler_params(("parallel",)),
        name="norm_proj",
    )(x, g, w, cos, sin)


def _stickbreak_kernel(q_ref, k_ref, v_ref, uu_ref, o_ref, acc_ref, carry_ref, *, tq, tk):
    def tile(qi, carry, first_tile=False):
        rows = pl.ds(pl.multiple_of(qi * tq, tq), tq)
        _stickbreak_tile(qi, q_ref.at[rows], k_ref, v_ref, uu_ref, o_ref.at[rows], acc_ref, carry_ref,
                         tq=tq, tk=tk, first_tile=first_tile)
        return carry

    tile(0, 0, first_tile=True)
    lax.fori_loop(1, q_ref.shape[0] // tq, tile, 0)


def _stickbreak_tile(qi, q_ref, k_ref, v_ref, uu_ref, o_ref, acc_ref, carry_ref, *, tq, tk, first_tile):
    head_lanes = _head_lanes()
    q = q_ref[...] * jnp.asarray(HEAD_DIM ** -0.5, _bf16)
    uu = uu_ref[...]
    strict = (lax.broadcasted_iota(jnp.int32, (tk, tk), 1)
              < lax.broadcasted_iota(jnp.int32, (tk, tk), 0))
    sign_bit = jnp.uint32(0x80000000)

    def mask_top(a, diagonal):
        if not diagonal:
            return a
        top = jnp.where(strict, a[:tk], 0.0)
        return top if a.shape[0] == tk else jnp.concatenate([top, a[tk:]], axis=0)

    def add_rows(x, r0, r1, delta):
        parts = ([x[:r0]] if r0 else []) + [x[r0:r1] + delta] + ([x[r1:]] if r1 < tq else [])
        return parts[0] if len(parts) == 1 else jnp.concatenate(parts, axis=0)

    def key_blocks(blocks, state):
        carries, acc = list(state[:2]), state[2]
        values, logits, afters = [], {}, {}
        for b, (jb, r0, r1, _) in enumerate(blocks):
            start = pl.multiple_of(jb * tk, tk)
            z = lax.dot_general(q[r0:r1], _by_head(k_ref[pl.ds(start, tk), :], head_lanes),
                                (((1,), (1,)), ((), ())), preferred_element_type=_f32)
            logits[b, 0], logits[b, 1] = z[:, :tk] * LOG2_E, z[:, tk:] * LOG2_E
            values.append(_by_head(v_ref[pl.ds(start, tk), :], head_lanes))
        for b, (_, _, _, diagonal) in enumerate(blocks):
            for h in range(2):
                z = logits[b, h]
                neg_abs = lax.bitcast_convert_type(lax.bitcast_convert_type(z, jnp.uint32) | sign_bit, _f32)
                softplus = jnp.maximum(z, 0.0) + jnp.log2(1.0 + jnp.exp2(neg_abs))
                sp = mask_top(softplus, diagonal)
                hi = sp.astype(_bf16)
                lo = (sp - hi.astype(_f32)).astype(_bf16)
                afters[b, h] = jnp.dot(jnp.concatenate([hi, lo], axis=1), uu,
                                       preferred_element_type=_f32)
        for b, (_, r0, r1, diagonal) in enumerate(blocks):
            weights = []
            for h in range(2):
                after = afters[b, h]
                w = mask_top(jnp.exp2(logits[b, h] + after + carries[h][r0:r1]), diagonal)
                weights.append(w.astype(_bf16))
                carries[h] = add_rows(carries[h], r0, r1, jnp.broadcast_to(after[:, :1], after.shape))
            pv = jnp.dot(jnp.concatenate(weights, axis=1), values[b], preferred_element_type=_f32)
            acc = add_rows(acc, r0, r1, pv)
        return carries[0], carries[1], acc

    blocks_per_tile = tq // tk
    first_diag = qi * blocks_per_tile

    def near_keys(with_previous_tile):
        zero = jnp.zeros((tq, LANES), _f32)
        blocks = [(first_diag + d, d * tk, tq, True) for d in reversed(range(blocks_per_tile))]
        if with_previous_tile:
            blocks += [(first_diag - 1 - d, 0, skipped_from[d], False)
                       for d in range(blocks_per_tile) if skipped_from[d]]
        return key_blocks(blocks, (zero, zero, zero))

    skipped_from = [min(tq, max(NEAR_KEYS - d * tk, 0)) for d in range(blocks_per_tile)]
    partly_skipped = [d for d in range(blocks_per_tile) if skipped_from[d] < tq]

    def carry_max(state, r0=0):
        return jnp.max(jnp.maximum(state[0][r0:], state[1][r0:]))

    def save(state):
        carry_ref[0], carry_ref[1], acc_ref[...] = state
        return state

    def load():
        return carry_ref[0], carry_ref[1], acc_ref[...]

    state = near_keys(not first_tile)
    o_ref[...] = state[2].astype(o_ref.dtype)
    if first_tile:
        return
    c_max = carry_max(state)

    def more(loop):
        t, c_max = loop
        return (t < qi) & (c_max > EXP2_UNDERFLOW)

    def older_keys(loop):
        t, _ = loop
        blocks = [(first_diag - 1 - t * blocks_per_tile - d, 0, tq, False) for d in range(blocks_per_tile)]
        return t + 1, carry_max(save(key_blocks(blocks, load())))

    @pl.when(c_max > EXP2_UNDERFLOW)
    def _():
        save(state)
        for d in partly_skipped:
            save(key_blocks([(first_diag - 1 - d, skipped_from[d], tq, False)], load()))
        lax.while_loop(more, older_keys, (jnp.int32(1), c_max))
        o_ref[...] = acc_ref[...].astype(o_ref.dtype)


def _cumsum_matrix(tk):
    src = jnp.arange(2 * tk)[:, None] % tk
    dst = jnp.arange(tk)[None, :]
    return -(src >= dst).astype(_bf16)


def _stickbreak(qkv, *, batch, seq, d_model, tq=256, tk=128):
    pairs = d_model // LANES
    return pl.pallas_call(
        functools.partial(_stickbreak_kernel, tq=tq, tk=tk),
        grid=(batch, pairs),
        in_specs=[
            pl.BlockSpec((None, seq, LANES), lambda b, p: (b, 0, p)),
            pl.BlockSpec((None, seq, LANES), lambda b, p: (b, 0, pairs + p)),
            pl.BlockSpec((None, seq, LANES), lambda b, p: (b, 0, 2 * pairs + p)),
            pl.BlockSpec((2 * tk, tk), lambda b, p: (0, 0)),
        ],
        out_specs=pl.BlockSpec((None, seq, LANES), lambda b, p: (b, 0, p)),
        out_shape=jax.ShapeDtypeStruct((batch, seq, d_model), _bf16),
        scratch_shapes=[pltpu.VMEM((tq, LANES), _f32), pltpu.VMEM((2, tq, tk), _f32)],
        compiler_params=_compiler_params(("parallel", "parallel")),
        name="stickbreak",
    )(qkv, qkv, qkv, _cumsum_matrix(tk))


def _dilated_kernel(q_ref, k_ref, v_ref, o_ref, m_ref, l_ref, acc_ref, bias_ref, *, group):
    per_class = q_ref.shape[1]
    tile_start = pl.program_id(2) * per_class
    head_lanes = _head_lanes()
    scale = jnp.asarray(HEAD_DIM ** -0.5, _bf16)
    row = lax.broadcasted_iota(jnp.int32, (BLOCK, 2 * BLOCK), 0)
    col = lax.broadcasted_iota(jnp.int32, (BLOCK, 2 * BLOCK), 1)
    in_prev = col < BLOCK
    ones_by_head = ((lax.broadcasted_iota(jnp.int32, (4 * BLOCK, LANES), 0) < 2 * BLOCK)
                    == (lax.broadcasted_iota(jnp.int32, (4 * BLOCK, LANES), 1) < HEAD_DIM)
                    ).astype(_f32).astype(_bf16)

    for branch, (window, r) in enumerate(DILATED_BRANCHES):
        assert window // r == BLOCK
        n_sub = N_CLASSES // r
        chunk = BLOCK // n_sub
        blocks_per_class = per_class // chunk
        sub_index = lambda a: n_sub * (a % chunk) + a // chunk
        q_sub, k_sub = sub_index(row), sub_index(col % BLOCK)
        band_prev = in_prev & (k_sub >= q_sub)
        band_cur = (~in_prev) & (k_sub <= q_sub)
        bias_ref[branch, 0] = jnp.where(band_cur, 0.0, MASK_VALUE)
        bias_ref[branch, 1] = jnp.where(band_prev | band_cur, 0.0, MASK_VALUE)

        def block_group(g, carry, r=r, branch=branch, n_sub=n_sub, chunk=chunk,
                        blocks_per_class=blocks_per_class):
            places, vs, biases, logits, probs, maxes = [], [], [], {}, {}, {}
            for u in range(group):
                idx = g * group + u
                c_r = idx // blocks_per_class
                local = pl.multiple_of((idx % blocks_per_class) * chunk, chunk)
                cur = tile_start + local
                has_prev = cur >= chunk
                prev = jnp.maximum(cur - chunk, 0)
                classes = [c_r + r * m for m in range(n_sub)]

                def rows(ref, start):
                    return jnp.concatenate([ref[c, pl.ds(start, chunk), :] for c in classes], axis=0)

                q = rows(q_ref, local).astype(_bf16) * scale
                k = jnp.concatenate([rows(k_ref, prev), rows(k_ref, cur)], axis=0).astype(_bf16)
                v = jnp.concatenate([rows(v_ref, prev), rows(v_ref, cur)], axis=0).astype(_bf16)
                places.append((classes, local))
                vs.append(jnp.concatenate([_by_head(v, head_lanes), ones_by_head], axis=1))
                biases.append(bias_ref[branch, has_prev.astype(jnp.int32)])
                for h in range(2):
                    q_head = jnp.where(head_lanes[h], q, jnp.zeros_like(q))
                    logits[u, h] = lax.dot_general(q_head, k, (((1,), (1,)), ((), ())),
                                                   preferred_element_type=_f32)
            for u in range(group):
                for h in range(2):
                    z = logits[u, h] + biases[u]
                    maxes[u, h] = jnp.max(z, axis=1, keepdims=True)
                    probs[u, h] = jnp.exp(z - maxes[u, h]).astype(_bf16)
            for u in range(group):
                classes, local = places[u]
                pv_l = jnp.dot(jnp.concatenate([probs[u, 0], probs[u, 1]], axis=1), vs[u],
                               preferred_element_type=_f32)
                pv_t, l_t = pv_l[:, :LANES], pv_l[:, LANES:]
                m_t = jnp.where(head_lanes[0], maxes[u, 0], maxes[u, 1])
                for m, c in enumerate(classes):
                    part = slice(m * chunk, (m + 1) * chunk)
                    place = (c, pl.ds(local, chunk), slice(None))
                    if branch == 0:
                        m_ref[place] = m_t[part]
                        l_ref[place] = l_t[part]
                        acc_ref[place] = pv_t[part]
                    else:
                        m_old = m_ref[place]
                        m_new = jnp.maximum(m_old, m_t[part])
                        a_old = jnp.exp(m_old - m_new)
                        a_new = jnp.exp(m_t[part] - m_new)
                        m_ref[place] = m_new
                        l_ref[place] = a_old * l_ref[place] + a_new * l_t[part]
                        acc_ref[place] = a_old * acc_ref[place] + a_new * pv_t[part]
            return carry

        lax.fori_loop(0, r * blocks_per_class // group, block_group, 0)

    o_ref[...] = (acc_ref[...] / l_ref[...]).astype(o_ref.dtype)


def _dilated(q, kv, *, batch, seq, d_model, group=4):
    pairs = d_model // LANES
    per_class = seq // N_CLASSES
    assert per_class % BLOCK == 0 and N_CLASSES % group == 0
    return pl.pallas_call(
        functools.partial(_dilated_kernel, group=group),
        grid=(batch, pairs, per_class // BLOCK),
        in_specs=[
            pl.BlockSpec((None, N_CLASSES, BLOCK, LANES), lambda b, p, i: (b, 0, i, p)),
            pl.BlockSpec((None, N_CLASSES, per_class, LANES), lambda b, p, i: (b, 0, 0, p)),
            pl.BlockSpec((None, N_CLASSES, per_class, LANES), lambda b, p, i: (b, 0, 0, pairs + p)),
        ],
        out_specs=pl.BlockSpec((None, N_CLASSES, BLOCK, LANES), lambda b, p, i: (b, 0, i, p)),
        out_shape=jax.ShapeDtypeStruct((batch, N_CLASSES, per_class, d_model), _bf16),
        scratch_shapes=[pltpu.VMEM((N_CLASSES, BLOCK, LANES), _f32)] * 3
        + [pltpu.VMEM((len(DILATED_BRANCHES), 2, BLOCK, 2 * BLOCK), _f32)],
        compiler_params=_compiler_params(("parallel", "parallel", "arbitrary")),
        name="dilated",
    )(q, kv, kv)


def _post_attn_kernel(h_ref, o_ref, wo_ref, g_ref, wg_ref, wu_ref, wd_ref, gf_ref, out_ref, acc_ref,
                      *, tf, final_norm):
    h1 = h_ref[...] + jnp.dot(o_ref[...], wo_ref[...], preferred_element_type=_f32)
    xn = _rms(h1, g_ref[...]).astype(_bf16)
    acc_ref[...] = h1
    for c in range(wg_ref.shape[1] // tf):
        cols = slice(c * tf, (c + 1) * tf)
        gate = jnp.dot(xn, wg_ref[:, cols], preferred_element_type=_f32)
        up = jnp.dot(xn, wu_ref[:, cols], preferred_element_type=_f32)
        act = (gate / (1.0 + jnp.exp(-gate)) * up).astype(_bf16)
        acc_ref[...] += jnp.dot(act, wd_ref[cols, :], preferred_element_type=_f32)
    out = acc_ref[...]
    if final_norm:
        out = _rms(out, gf_ref[...])
    out_ref[...] = out


def _post_attn(h, o, wo, g, wg, wu, wd, g_final, *, final_norm, tm=512, tf=256):
    m, d = h.shape
    f = wg.shape[1]
    resident = lambda shape: pl.BlockSpec(shape, lambda i: (0, 0), pipeline_mode=pl.Buffered(1))
    return pl.pallas_call(
        functools.partial(_post_attn_kernel, tf=tf, final_norm=final_norm),
        grid=(m // tm,),
        in_specs=[
            pl.BlockSpec((tm, d), lambda i: (i, 0)),
            pl.BlockSpec((tm, d), lambda i: (i, 0)),
            resident((d, d)),
            resident((1, d)),
            resident((d, f)),
            resident((d, f)),
            resident((f, d)),
            resident((1, d)),
        ],
        out_specs=pl.BlockSpec((tm, d), lambda i: (i, 0)),
        out_shape=jax.ShapeDtypeStruct((m, d), _f32),
        scratch_shapes=[pltpu.VMEM((tm, d), _f32)],
        compiler_params=_compiler_params(("parallel",)),
        name="post_attn",
    )(h, o, wo, g, wg, wu, wd, g_final)


def _rope_tables(seq):
    inv_freq = ROPE_THETA ** (-jnp.arange(0, HEAD_DIM, 2, dtype=_f32) / HEAD_DIM)
    ang = jnp.arange(seq, dtype=_f32)[:, None] * inv_freq[None, :]
    cos, sin = jnp.cos(ang), jnp.sin(ang)
    reps = LANES // HEAD_DIM
    return (jnp.tile(jnp.concatenate([cos, cos], axis=1), (1, reps)),
            jnp.tile(jnp.concatenate([-sin, sin], axis=1), (1, reps)))


def _class_major(a, lead):
    s, f = a.shape[-2:]
    return jnp.swapaxes(a.reshape(*lead, s // N_CLASSES, N_CLASSES, f), -3, -2)


def kernel(x, norm_mix, w_qkv_a, w_o_a, norm_kv, w_kv, w_q_b, w_o_b, norm_ffn, w_gate, w_up, w_down,
           norm_final):
    b, s, d = x.shape
    depth = norm_mix.shape[0]
    n_a = w_qkv_a.shape[0]
    assert d % LANES == 0 and s % (N_CLASSES * BLOCK) == 0
    cos, sin = _rope_tables(s)
    cos_cm, sin_cm = (_class_major(t, ()).reshape(s, LANES) for t in (cos, sin))
    cast = lambda w: w.astype(_bf16)
    w_qkv_a, w_o_a, w_kv, w_q_b, w_o_b = map(cast, (w_qkv_a, w_o_a, w_kv, w_q_b, w_o_b))
    w_gate, w_up, w_down = map(cast, (w_gate, w_up, w_down))
    g_final = norm_final.reshape(1, d)
    per_class = s // N_CLASSES

    h = x.reshape(b * s, d)
    kv = None
    for layer in range(depth):
        g_mix = norm_mix[layer].reshape(1, d)
        if layer < n_a:
            qkv = _norm_proj(h, g_mix, w_qkv_a[layer], cos, sin, rope_cols=0, out_dtype=_bf16, seq=s)
            o = _stickbreak(qkv.reshape(b, s, 3 * d), batch=b, seq=s, d_model=d)
            w_o = w_o_a[layer]
        else:
            j = layer - n_a
            if kv is None:
                h = _class_major(h.reshape(b, s, d), (b,)).reshape(b * s, d)
                kv = _norm_proj(h, norm_kv.reshape(1, d), w_kv, cos_cm, sin_cm, rope_cols=d, out_dtype=_f32,
                                seq=s).reshape(b, N_CLASSES, per_class, 2 * d)
            q = _norm_proj(h, g_mix, w_q_b[j], cos_cm, sin_cm, rope_cols=d, out_dtype=_f32, seq=s)
            o = _dilated(q.reshape(b, N_CLASSES, per_class, d), kv, batch=b, seq=s, d_model=d)
            w_o = w_o_b[j]
        h = _post_attn(h, o.reshape(b * s, d), w_o, norm_ffn[layer].reshape(1, d), w_gate[layer],
                       w_up[layer], w_down[layer], g_final, final_norm=(layer == depth - 1))
    if kv is not None:
        h = jnp.swapaxes(h.reshape(b, N_CLASSES, per_class, d), 1, 2)
    return h.reshape(b, s, d)
```

```python
import functools

import jax
import jax.numpy as jnp
from jax import lax
from jax.experimental import pallas as pl
from jax.experimental.pallas import tpu as pltpu

HEAD_DIM = 64
BLOCK = 128
ROPE_THETA = 10000.0
NORM_EPS = 1e-6
DILATED_BRANCHES = ((128, 1), (512, 4), (2048, 16))
N_CLASSES = max(r for _, r in DILATED_BRANCHES)

LANES = 128
VMEM_LIMIT_BYTES = 56 * 1024 * 1024
MASK_VALUE = -1e30
EXP2_UNDERFLOW = -160.0
LOG2_E = 1.4426950408889634
NEAR_KEYS = 176
MXU_WIDTH = 256

_f32 = jnp.float32
_bf16 = jnp.bfloat16


def _rms(x, g):
    return x * lax.rsqrt(jnp.mean(x * x, axis=-1, keepdims=True) + NORM_EPS) * g


def _compiler_params(semantics):
    return pltpu.CompilerParams(dimension_semantics=semantics, vmem_limit_bytes=VMEM_LIMIT_BYTES)


def _head_lanes():
    lane = lax.broadcasted_iota(jnp.int32, (1, LANES), 1)
    return lane < HEAD_DIM, lane >= HEAD_DIM


def _by_head(x, head_lanes):
    return jnp.concatenate([jnp.where(hm, x, jnp.zeros_like(x)) for hm in head_lanes], axis=0)


def _norm_proj_kernel(x_ref, g_ref, w_ref, cos_ref, sin_ref, o_ref, *, rope_cols):
    xn = _rms(x_ref[...], g_ref[...]).astype(_bf16)
    n_out = o_ref.shape[1]
    tn = MXU_WIDTH
    if rope_cols:
        reps = tn // LANES
        cos = jnp.concatenate([cos_ref[...]] * reps, axis=1)
        sin = jnp.concatenate([sin_ref[...]] * reps, axis=1)
        first_half = (lax.broadcasted_iota(jnp.int32, cos.shape, 1) % HEAD_DIM) < HEAD_DIM // 2
    for c in range(n_out // tn):
        cols = slice(c * tn, (c + 1) * tn)
        y = jnp.dot(xn, w_ref[:, cols], preferred_element_type=_f32)
        if c * tn < rope_cols:
            partner = jnp.where(first_half, pltpu.roll(y, tn - HEAD_DIM // 2, 1),
                                pltpu.roll(y, HEAD_DIM // 2, 1))
            y = y * cos + partner * sin
        o_ref[:, cols] = y.astype(o_ref.dtype)


def _norm_proj(x, g, w, cos, sin, *, rope_cols, out_dtype, seq, tm=512):
    m, d = x.shape
    n_out = w.shape[1]
    pos_blocks = seq // tm
    return pl.pallas_call(
        functools.partial(_norm_proj_kernel, rope_cols=rope_cols),
        grid=(m // tm,),
        in_specs=[
            pl.BlockSpec((tm, d), lambda i: (i, 0)),
            pl.BlockSpec((1, d), lambda i: (0, 0)),
            pl.BlockSpec((d, n_out), lambda i: (0, 0)),
            pl.BlockSpec((tm, LANES), lambda i: (i % pos_blocks, 0)),
            pl.BlockSpec((tm, LANES), lambda i: (i % pos_blocks, 0)),
        ],
        out_specs=pl.BlockSpec((tm, n_out), lambda i: (i, 0)),
        out_shape=jax.ShapeDtypeStruct((m, n_out), out_dtype),
        compiler_params=_compiler_params(("parallel",)),
        name="norm_proj",
    )(x, g, w, cos, sin)


def _stickbreak_kernel(q_ref, k_ref, v_ref, uu_ref, o_ref, acc_ref, carry_ref, *, tq, tk):
    def tile(qi, carry, first_tile=False):
        rows = pl.ds(pl.multiple_of(qi * tq, tq), tq)
        _stickbreak_tile(qi, q_ref.at[rows], k_ref, v_ref, uu_ref, o_ref.at[rows], acc_ref, carry_ref,
                         tq=tq, tk=tk, first_tile=first_tile)
        return carry

    tile(0, 0, first_tile=True)
    lax.fori_loop(1, q_ref.shape[0] // tq, tile, 0)


def _stickbreak_tile(qi, q_ref, k_ref, v_ref, uu_ref, o_ref, acc_ref, carry_ref, *, tq, tk, first_tile):
    head_lanes = _head_lanes()
    q = q_ref[...] * jnp.asarray(HEAD_DIM ** -0.5, _bf16)
    uu = uu_ref[...]
    strict = (lax.broadcasted_iota(jnp.int32, (tk, tk), 1)
              < lax.broadcasted_iota(jnp.int32, (tk, tk), 0))
    sign_bit = jnp.uint32(0x80000000)

    def mask_top(a, diagonal):
        if not diagonal:
            return a
        top = jnp.where(strict, a[:tk], 0.0)
        return top if a.shape[0] == tk else jnp.concatenate([top, a[tk:]], axis=0)

    def add_rows(x, r0, r1, delta):
        parts = ([x[:r0]] if r0 else []) + [x[r0:r1] + delta] + ([x[r1:]] if r1 < tq else [])
        return parts[0] if len(parts) == 1 else jnp.concatenate(parts, axis=0)

    def key_blocks(blocks, state):
        carries, acc = list(state[:2]), state[2]
        values, logits, afters = [], {}, {}
        for b, (jb, r0, r1, _) in enumerate(blocks):
            start = pl.multiple_of(jb * tk, tk)
            z = lax.dot_general(q[r0:r1], _by_head(k_ref[pl.ds(start, tk), :], head_lanes),
                                (((1,), (1,)), ((), ())), preferred_element_type=_f32)
            logits[b, 0], logits[b, 1] = z[:, :tk] * LOG2_E, z[:, tk:] * LOG2_E
            values.append(_by_head(v_ref[pl.ds(start, tk), :], head_lanes))
        for b, (_, _, _, diagonal) in enumerate(blocks):
            for h in range(2):
                z = logits[b, h]
                neg_abs = lax.bitcast_convert_type(lax.bitcast_convert_type(z, jnp.uint32) | sign_bit, _f32)
                softplus = jnp.maximum(z, 0.0) + jnp.log2(1.0 + jnp.exp2(neg_abs))
                sp = mask_top(softplus, diagonal)
                hi = sp.astype(_bf16)
                lo = (sp - hi.astype(_f32)).astype(_bf16)
                afters[b, h] = jnp.dot(jnp.concatenate([hi, lo], axis=1), uu,
                                       preferred_element_type=_f32)
        for b, (_, r0, r1, diagonal) in enumerate(blocks):
            weights = []
            for h in range(2):
                after = afters[b, h]
                w = mask_top(jnp.exp2(logits[b, h] + after + carries[h][r0:r1]), diagonal)
                weights.append(w.astype(_bf16))
                carries[h] = add_rows(carries[h], r0, r1, jnp.broadcast_to(after[:, :1], after.shape))
            pv = jnp.dot(jnp.concatenate(weights, axis=1), values[b], preferred_element_type=_f32)
            acc = add_rows(acc, r0, r1, pv)
        return carries[0], carries[1], acc

    blocks_per_tile = tq // tk
    first_diag = qi * blocks_per_tile

    def near_keys(with_previous_tile):
        zero = jnp.zeros((tq, LANES), _f32)
        blocks = [(first_diag + d, d * tk, tq, True) for d in reversed(range(blocks_per_tile))]
        if with_previous_tile:
            blocks += [(first_diag - 1 - d, 0, skipped_from[d], False)
                       for d in range(blocks_per_tile) if skipped_from[d]]
        return key_blocks(blocks, (zero, zero, zero))

    skipped_from = [min(tq, max(NEAR_KEYS - d * tk, 0)) for d in range(blocks_per_tile)]
    partly_skipped = [d for d in range(blocks_per_tile) if skipped_from[d] < tq]

    def carry_max(state, r0=0):
        return jnp.max(jnp.maximum(state[0][r0:], state[1][r0:]))

    def save(state):
        carry_ref[0], carry_ref[1], acc_ref[...] = state
        return state

    def load():
        return carry_ref[0], carry_ref[1], acc_ref[...]

    state = near_keys(not first_tile)
    o_ref[...] = state[2].astype(o_ref.dtype)
    if first_tile:
        return
    c_max = carry_max(state)

    def more(loop):
        t, c_max = loop
        return (t < qi) & (c_max > EXP2_UNDERFLOW)

    def older_keys(loop):
        t, _ = loop
        blocks = [(first_diag - 1 - t * blocks_per_tile - d, 0, tq, False) for d in range(blocks_per_tile)]
        return t + 1, carry_max(save(key_blocks(blocks, load())))

    @pl.when(c_max > EXP2_UNDERFLOW)
    def _():
        save(state)
        for d in partly_skipped:
            save(key_blocks([(first_diag - 1 - d, skipped_from[d], tq, False)], load()))
        lax.while_loop(more, older_keys, (jnp.int32(1), c_max))
        o_ref[...] = acc_ref[...].astype(o_ref.dtype)


def _cumsum_matrix(tk):
    src = jnp.arange(2 * tk)[:, None] % tk
    dst = jnp.arange(tk)[None, :]
    return -(src >= dst).astype(_bf16)


def _stickbreak(qkv, *, batch, seq, d_model, tq=256, tk=128):
    pairs = d_model // LANES
    return pl.pallas_call(
        functools.partial(_stickbreak_kernel, tq=tq, tk=tk),
        grid=(batch, pairs),
        in_specs=[
            pl.BlockSpec((None, seq, LANES), lambda b, p: (b, 0, p)),
            pl.BlockSpec((None, seq, LANES), lambda b, p: (b, 0, pairs + p)),
            pl.BlockSpec((None, seq, LANES), lambda b, p: (b, 0, 2 * pairs + p)),
            pl.BlockSpec((2 * tk, tk), lambda b, p: (0, 0)),
        ],
        out_specs=pl.BlockSpec((None, seq, LANES), lambda b, p: (b, 0, p)),
        out_shape=jax.ShapeDtypeStruct((batch, seq, d_model), _bf16),
        scratch_shapes=[pltpu.VMEM((tq, LANES), _f32), pltpu.VMEM((2, tq, tk), _f32)],
        compiler_params=_compiler_params(("parallel", "parallel")),
        name="stickbreak",
    )(qkv, qkv, qkv, _cumsum_matrix(tk))


def _dilated_kernel(q_ref, k_ref, v_ref, o_ref, m_ref, l_ref, acc_ref, bias_ref, *, group):
    per_class = q_ref.shape[1]
    tile_start = pl.program_id(2) * per_class
    head_lanes = _head_lanes()
    scale = jnp.asarray(HEAD_DIM ** -0.5, _bf16)
    row = lax.broadcasted_iota(jnp.int32, (BLOCK, 2 * BLOCK), 0)
    col = lax.broadcasted_iota(jnp.int32, (BLOCK, 2 * BLOCK), 1)
    in_prev = col < BLOCK
    ones_by_head = ((lax.broadcasted_iota(jnp.int32, (4 * BLOCK, LANES), 0) < 2 * BLOCK)
                    == (lax.broadcasted_iota(jnp.int32, (4 * BLOCK, LANES), 1) < HEAD_DIM)
                    ).astype(_f32).astype(_bf16)
    stages = []

    for branch, (window, r) in enumerate(DILATED_BRANCHES):
        assert window // r == BLOCK
        n_sub = N_CLASSES // r
        chunk = BLOCK // n_sub
        blocks_per_class = per_class // chunk
        sub_index = lambda a: n_sub * (a % chunk) + a // chunk
        q_sub, k_sub = sub_index(row), sub_index(col % BLOCK)
        band_prev = in_prev & (k_sub >= q_sub)
        band_cur = (~in_prev) & (k_sub <= q_sub)
        bias_ref[branch, 0] = jnp.where(band_cur, 0.0, MASK_VALUE)
        bias_ref[branch, 1] = jnp.where(band_prev | band_cur, 0.0, MASK_VALUE)

        def score_stage(g, r=r, branch=branch, n_sub=n_sub, chunk=chunk, blocks_per_class=blocks_per_class):
            places, vs, biases, logits = [], [], [], {}
            for u in range(group):
                idx = g * group + u
                c_r = idx // blocks_per_class
                local = (idx % blocks_per_class) * chunk
                cur = tile_start + local
                prev = jnp.maximum(cur - chunk, 0)
                classes = [c_r + r * m for m in range(n_sub)]

                def rows(ref, start):
                    start = start if isinstance(start, int) else pl.multiple_of(start, chunk)
                    return jnp.concatenate([ref[c, pl.ds(start, chunk), :] for c in classes], axis=0)

                q = rows(q_ref, local).astype(_bf16) * scale
                k = jnp.concatenate([rows(k_ref, prev), rows(k_ref, cur)], axis=0).astype(_bf16)
                v = jnp.concatenate([rows(v_ref, prev), rows(v_ref, cur)], axis=0).astype(_bf16)
                places.append((classes, local))
                vs.append(jnp.concatenate([_by_head(v, head_lanes), ones_by_head], axis=1))
                biases.append(bias_ref[branch, 1] if local >= chunk
                              else bias_ref[branch, (cur >= chunk).astype(jnp.int32)])
                for h in range(2):
                    q_head = jnp.where(head_lanes[h], q, jnp.zeros_like(q))
                    logits[u, h] = lax.dot_general(q_head, k, (((1,), (1,)), ((), ())),
                                                   preferred_element_type=_f32)
            return branch, chunk, places, vs, biases, logits

        stages += [functools.partial(score_stage, g) for g in range(r * blocks_per_class // group)]

    def finish_stage(branch, chunk, places, vs, biases, logits):
        probs, maxes = {}, {}
        for u in range(group):
            for h in range(2):
                z = logits[u, h] + biases[u]
                maxes[u, h] = jnp.max(z, axis=1, keepdims=True)
                probs[u, h] = jnp.exp(z - maxes[u, h]).astype(_bf16)
        for u in range(group):
            classes, local = places[u]
            pv_l = jnp.dot(jnp.concatenate([probs[u, 0], probs[u, 1]], axis=1), vs[u],
                           preferred_element_type=_f32)
            pv_t, l_t = pv_l[:, :LANES], pv_l[:, LANES:]
            m_t = jnp.where(head_lanes[0], maxes[u, 0], maxes[u, 1])
            for m, c in enumerate(classes):
                part = slice(m * chunk, (m + 1) * chunk)
                place = (c, pl.ds(local, chunk), slice(None))
                if branch == 0:
                    m_ref[place] = m_t[part]
                    l_ref[place] = l_t[part]
                    acc_ref[place] = pv_t[part]
                else:
                    m_old = m_ref[place]
                    m_new = jnp.maximum(m_old, m_t[part])
                    a_old = jnp.exp(m_old - m_new)
                    a_new = jnp.exp(m_t[part] - m_new)
                    m_ref[place] = m_new
                    l_ref[place] = a_old * l_ref[place] + a_new * l_t[part]
                    acc_ref[place] = a_old * acc_ref[place] + a_new * pv_t[part]

    scores = stages[0]()
    for nxt in stages[1:] + [None]:
        ahead = nxt() if nxt is not None else None
        finish_stage(*scores)
        scores = ahead

    o_ref[...] = (acc_ref[...] / l_ref[...]).astype(o_ref.dtype)


def _dilated(q, kv, *, batch, seq, d_model, group=4):
    pairs = d_model // LANES
    per_class = seq // N_CLASSES
    assert per_class % BLOCK == 0 and N_CLASSES % group == 0
    return pl.pallas_call(
        functools.partial(_dilated_kernel, group=group),
        grid=(batch, pairs, per_class // BLOCK),
        in_specs=[
            pl.BlockSpec((None, N_CLASSES, BLOCK, LANES), lambda b, p, i: (b, 0, i, p)),
            pl.BlockSpec((None, N_CLASSES, per_class, LANES), lambda b, p, i: (b, 0, 0, p)),
            pl.BlockSpec((None, N_CLASSES, per_class, LANES), lambda b, p, i: (b, 0, 0, pairs + p)),
        ],
        out_specs=pl.BlockSpec((None, N_CLASSES, BLOCK, LANES), lambda b, p, i: (b, 0, i, p)),
        out_shape=jax.ShapeDtypeStruct((batch, N_CLASSES, per_class, d_model), _bf16),
        scratch_shapes=[pltpu.VMEM((N_CLASSES, BLOCK, LANES), _f32)] * 3
        + [pltpu.VMEM((len(DILATED_BRANCHES), 2, BLOCK, 2 * BLOCK), _f32)],
        compiler_params=_compiler_params(("parallel", "parallel", "arbitrary")),
        name="dilated",
    )(q, kv, kv)


def _post_attn_kernel(h_ref, o_ref, wo_ref, g_ref, wg_ref, wu_ref, wd_ref, gf_ref, out_ref, acc_ref,
                      *, tf, final_norm):
    h1 = h_ref[...] + jnp.dot(o_ref[...], wo_ref[...], preferred_element_type=_f32)
    xn = _rms(h1, g_ref[...]).astype(_bf16)
    acc_ref[...] = h1
    for c in range(wg_ref.shape[1] // tf):
        cols = slice(c * tf, (c + 1) * tf)
        gate = jnp.dot(xn, wg_ref[:, cols], preferred_element_type=_f32)
        up = jnp.dot(xn, wu_ref[:, cols], preferred_element_type=_f32)
        act = (gate / (1.0 + jnp.exp(-gate)) * up).astype(_bf16)
        acc_ref[...] += jnp.dot(act, wd_ref[cols, :], preferred_element_type=_f32)
    out = acc_ref[...]
    if final_norm:
        out = _rms(out, gf_ref[...])
    out_ref[...] = out


def _post_attn(h, o, wo, g, wg, wu, wd, g_final, *, final_norm, tm=512, tf=256):
    m, d = h.shape
    f = wg.shape[1]
    resident = lambda shape: pl.BlockSpec(shape, lambda i: (0, 0), pipeline_mode=pl.Buffered(1))
    return pl.pallas_call(
        functools.partial(_post_attn_kernel, tf=tf, final_norm=final_norm),
        grid=(m // tm,),
        in_specs=[
            pl.BlockSpec((tm, d), lambda i: (i, 0)),
            pl.BlockSpec((tm, d), lambda i: (i, 0)),
            resident((d, d)),
            resident((1, d)),
            resident((d, f)),
            resident((d, f)),
            resident((f, d)),
            resident((1, d)),
        ],
        out_specs=pl.BlockSpec((tm, d), lambda i: (i, 0)),
        out_shape=jax.ShapeDtypeStruct((m, d), _f32),
        scratch_shapes=[pltpu.VMEM((tm, d), _f32)],
        compiler_params=_compiler_params(("parallel",)),
        name="post_attn",
    )(h, o, wo, g, wg, wu, wd, g_final)


def _rope_tables(seq):
    inv_freq = ROPE_THETA ** (-jnp.arange(0, HEAD_DIM, 2, dtype=_f32) / HEAD_DIM)
    ang = jnp.arange(seq, dtype=_f32)[:, None] * inv_freq[None, :]
    cos, sin = jnp.cos(ang), jnp.sin(ang)
    reps = LANES // HEAD_DIM
    return (jnp.tile(jnp.concatenate([cos, cos], axis=1), (1, reps)),
            jnp.tile(jnp.concatenate([-sin, sin], axis=1), (1, reps)))


def _class_major(a, lead):
    s, f = a.shape[-2:]
    return jnp.swapaxes(a.reshape(*lead, s // N_CLASSES, N_CLASSES, f), -3, -2)


def kernel(x, norm_mix, w_qkv_a, w_o_a, norm_kv, w_kv, w_q_b, w_o_b, norm_ffn, w_gate, w_up, w_down,
           norm_final):
    b, s, d = x.shape
    depth = norm_mix.shape[0]
    n_a = w_qkv_a.shape[0]
    assert d % LANES == 0 and s % (N_CLASSES * BLOCK) == 0
    cos, sin = _rope_tables(s)
    cos_cm, sin_cm = (_class_major(t, ()).reshape(s, LANES) for t in (cos, sin))
    cast = lambda w: w.astype(_bf16)
    w_qkv_a, w_o_a, w_kv, w_q_b, w_o_b = map(cast, (w_qkv_a, w_o_a, w_kv, w_q_b, w_o_b))
    w_gate, w_up, w_down = map(cast, (w_gate, w_up, w_down))
    g_final = norm_final.reshape(1, d)
    per_class = s // N_CLASSES

    h = x.reshape(b * s, d)
    kv = None
    for layer in range(depth):
        g_mix = norm_mix[layer].reshape(1, d)
        if layer < n_a:
            qkv = _norm_proj(h, g_mix, w_qkv_a[layer], cos, sin, rope_cols=0, out_dtype=_bf16, seq=s)
            o = _stickbreak(qkv.reshape(b, s, 3 * d), batch=b, seq=s, d_model=d)
            w_o = w_o_a[layer]
        else:
            j = layer - n_a
            if kv is None:
                h = _class_major(h.reshape(b, s, d), (b,)).reshape(b * s, d)
                kv = _norm_proj(h, norm_kv.reshape(1, d), w_kv, cos_cm, sin_cm, rope_cols=d, out_dtype=_f32,
                                seq=s).reshape(b, N_CLASSES, per_class, 2 * d)
            q = _norm_proj(h, g_mix, w_q_b[j], cos_cm, sin_cm, rope_cols=d, out_dtype=_f32, seq=s)
            o = _dilated(q.reshape(b, N_CLASSES, per_class, d), kv, batch=b, seq=s, d_model=d)
            w_o = w_o_b[j]
        h = _post_attn(h, o.reshape(b * s, d), w_o, norm_ffn[layer].reshape(1, d), w_gate[layer],
                       w_up[layer], w_down[layer], g_final, final_norm=(layer == depth - 1))
    if kv is not None:
        h = jnp.swapaxes(h.reshape(b, N_CLASSES, per_class, d), 1, 2)
    return h.reshape(b, s, d)
```

```python
import functools

import jax
import jax.numpy as jnp
from jax import lax
from jax.experimental import pallas as pl
from jax.experimental.pallas import tpu as pltpu

HEAD_DIM = 64
BLOCK = 128
ROPE_THETA = 10000.0
NORM_EPS = 1e-6
DILATED_BRANCHES = ((128, 1), (512, 4), (2048, 16))
N_CLASSES = max(r for _, r in DILATED_BRANCHES)

LANES = 128
VMEM_LIMIT_BYTES = 56 * 1024 * 1024
MASK_VALUE = -1e30
EXP2_UNDERFLOW = -160.0
LOG2_E = 1.4426950408889634
NEAR_KEYS = 176
MXU_WIDTH = 256

_f32 = jnp.float32
_bf16 = jnp.bfloat16


def _rms(x, g):
    return x * lax.rsqrt(jnp.mean(x * x, axis=-1, keepdims=True) + NORM_EPS) * g


def _compiler_params(semantics):
    return pltpu.CompilerParams(dimension_semantics=semantics, vmem_limit_bytes=VMEM_LIMIT_BYTES)


def _head_lanes():
    lane = lax.broadcasted_iota(jnp.int32, (1, LANES), 1)
    return lane < HEAD_DIM, lane >= HEAD_DIM


def _by_head(x, head_lanes):
    return jnp.concatenate([jnp.where(hm, x, jnp.zeros_like(x)) for hm in head_lanes], axis=0)


def _norm_proj_kernel(x_ref, g_ref, w_ref, cos_ref, sin_ref, o_ref, *, rope_cols):
    xn = _rms(x_ref[...], g_ref[...]).astype(_bf16)
    n_out = o_ref.shape[1]
    tn = MXU_WIDTH
    if rope_cols:
        reps = tn // LANES
        cos = jnp.concatenate([cos_ref[...]] * reps, axis=1)
        sin = jnp.concatenate([sin_ref[...]] * reps, axis=1)
        first_half = (lax.broadcasted_iota(jnp.int32, cos.shape, 1) % HEAD_DIM) < HEAD_DIM // 2
    for c in range(n_out // tn):
        cols = slice(c * tn, (c + 1) * tn)
        y = jnp.dot(xn, w_ref[:, cols], preferred_element_type=_f32)
        if c * tn < rope_cols:
            partner = jnp.where(first_half, pltpu.roll(y, tn - HEAD_DIM // 2, 1),
                                pltpu.roll(y, HEAD_DIM // 2, 1))
            y = y * cos + partner * sin
        o_ref[:, cols] = y.astype(o_ref.dtype)


def _norm_proj(x, g, w, cos, sin, *, rope_cols, out_dtype, seq, tm=512):
    m, d = x.shape
    n_out = w.shape[1]
    pos_blocks = seq // tm
    return pl.pallas_call(
        functools.partial(_norm_proj_kernel, rope_cols=rope_cols),
        grid=(m // tm,),
        in_specs=[
            pl.BlockSpec((tm, d), lambda i: (i, 0)),
            pl.BlockSpec((1, d), lambda i: (0, 0)),
            pl.BlockSpec((d, n_out), lambda i: (0, 0)),
            pl.BlockSpec((tm, LANES), lambda i: (i % pos_blocks, 0)),
            pl.BlockSpec((tm, LANES), lambda i: (i % pos_blocks, 0)),
        ],
        out_specs=pl.BlockSpec((tm, n_out), lambda i: (i, 0)),
        out_shape=jax.ShapeDtypeStruct((m, n_out), out_dtype),
        compiler_params=_compiler_params(("parallel",)),
        name="norm_proj",
    )(x, g, w, cos, sin)


def _stickbreak_kernel(q_ref, k_ref, v_ref, uu_ref, o_ref, acc_ref, carry_ref, *, tq, tk, per_trip):
    def start_tile(qi, first_tile=False):
        rows = pl.ds(qi * tq if first_tile else pl.multiple_of(qi * tq, tq), tq)
        return _stickbreak_tile(qi, q_ref.at[rows], k_ref, v_ref, uu_ref, o_ref.at[rows], acc_ref, carry_ref,
                                tq=tq, tk=tk, first_tile=first_tile)

    def tiles(trip, carry):
        first = 1 + trip * per_trip
        finish = start_tile(first)
        for j in range(per_trip):
            ahead = start_tile(first + j + 1) if j + 1 < per_trip else None
            finish()
            finish = ahead
        return carry

    n_tiles = q_ref.shape[0] // tq
    assert (n_tiles - 1) % per_trip == 0
    start_tile(0, first_tile=True)()
    lax.fori_loop(0, (n_tiles - 1) // per_trip, tiles, 0)


def _stickbreak_tile(qi, q_ref, k_ref, v_ref, uu_ref, o_ref, acc_ref, carry_ref, *, tq, tk, first_tile):
    head_lanes = _head_lanes()
    q = q_ref[...] * jnp.asarray(HEAD_DIM ** -0.5, _bf16)
    uu = uu_ref[...]
    strict = (lax.broadcasted_iota(jnp.int32, (tk, tk), 1)
              < lax.broadcasted_iota(jnp.int32, (tk, tk), 0))
    sign_bit = jnp.uint32(0x80000000)

    def mask_top(a, diagonal):
        if not diagonal:
            return a
        top = jnp.where(strict, a[:tk], 0.0)
        return top if a.shape[0] == tk else jnp.concatenate([top, a[tk:]], axis=0)

    def add_rows(x, r0, r1, delta):
        parts = ([x[:r0]] if r0 else []) + [x[r0:r1] + delta] + ([x[r1:]] if r1 < tq else [])
        return parts[0] if len(parts) == 1 else jnp.concatenate(parts, axis=0)

    def score_blocks(blocks):
        values, logits, afters = [], {}, {}
        for b, (jb, r0, r1, _) in enumerate(blocks):
            start = pl.multiple_of(jb * tk, tk)
            z = lax.dot_general(q[r0:r1], _by_head(k_ref[pl.ds(start, tk), :], head_lanes),
                                (((1,), (1,)), ((), ())), preferred_element_type=_f32)
            logits[b, 0], logits[b, 1] = z[:, :tk] * LOG2_E, z[:, tk:] * LOG2_E
            values.append(_by_head(v_ref[pl.ds(start, tk), :], head_lanes))
        for b, (_, _, _, diagonal) in enumerate(blocks):
            for h in range(2):
                z = logits[b, h]
                neg_abs = lax.bitcast_convert_type(lax.bitcast_convert_type(z, jnp.uint32) | sign_bit, _f32)
                softplus = jnp.maximum(z, 0.0) + jnp.log2(1.0 + jnp.exp2(neg_abs))
                sp = mask_top(softplus, diagonal)
                hi = sp.astype(_bf16)
                lo = (sp - hi.astype(_f32)).astype(_bf16)
                afters[b, h] = jnp.dot(jnp.concatenate([hi, lo], axis=1), uu,
                                       preferred_element_type=_f32)
        return values, logits, afters

    def weigh_blocks(blocks, scores, state):
        values, logits, afters = scores
        carries, acc = list(state[:2]), state[2]
        for b, (_, r0, r1, diagonal) in enumerate(blocks):
            weights = []
            for h in range(2):
                after = afters[b, h]
                w = mask_top(jnp.exp2(logits[b, h] + after + carries[h][r0:r1]), diagonal)
                weights.append(w.astype(_bf16))
                carries[h] = add_rows(carries[h], r0, r1, jnp.broadcast_to(after[:, :1], after.shape))
            pv = jnp.dot(jnp.concatenate(weights, axis=1), values[b], preferred_element_type=_f32)
            acc = add_rows(acc, r0, r1, pv)
        return carries[0], carries[1], acc

    def key_blocks(blocks, state):
        return weigh_blocks(blocks, score_blocks(blocks), state)

    blocks_per_tile = tq // tk
    first_diag = qi * blocks_per_tile

    skipped_from = [min(tq, max(NEAR_KEYS - d * tk, 0)) for d in range(blocks_per_tile)]
    partly_skipped = [d for d in range(blocks_per_tile) if skipped_from[d] < tq]

    near_blocks = [(first_diag + d, d * tk, tq, True) for d in reversed(range(blocks_per_tile))]
    if not first_tile:
        near_blocks += [(first_diag - 1 - d, 0, skipped_from[d], False)
                        for d in range(blocks_per_tile) if skipped_from[d]]
    near_scores = score_blocks(near_blocks)

    def carry_max(state, r0=0):
        return jnp.max(jnp.maximum(state[0][r0:], state[1][r0:]))

    def save(state):
        carry_ref[0], carry_ref[1], acc_ref[...] = state
        return state

    def load():
        return carry_ref[0], carry_ref[1], acc_ref[...]

    def more(loop):
        t, c_max = loop
        return (t < qi) & (c_max > EXP2_UNDERFLOW)

    def older_keys(loop):
        t, _ = loop
        blocks = [(first_diag - 1 - t * blocks_per_tile - d, 0, tq, False) for d in range(blocks_per_tile)]
        return t + 1, carry_max(save(key_blocks(blocks, load())))

    def finish():
        zero = jnp.zeros((tq, LANES), _f32)
        state = weigh_blocks(near_blocks, near_scores, (zero, zero, zero))
        o_ref[...] = state[2].astype(o_ref.dtype)
        if first_tile:
            return
        c_max = carry_max(state)

        @pl.when(c_max > EXP2_UNDERFLOW)
        def _():
            save(state)
            for d in partly_skipped:
                save(key_blocks([(first_diag - 1 - d, skipped_from[d], tq, False)], load()))
            lax.while_loop(more, older_keys, (jnp.int32(1), c_max))
            o_ref[...] = acc_ref[...].astype(o_ref.dtype)

    return finish


def _cumsum_matrix(tk):
    src = jnp.arange(2 * tk)[:, None] % tk
    dst = jnp.arange(tk)[None, :]
    return -(src >= dst).astype(_bf16)


def _stickbreak(qkv, *, batch, seq, d_model, tq=256, tk=128, per_trip=5):
    pairs = d_model // LANES
    return pl.pallas_call(
        functools.partial(_stickbreak_kernel, tq=tq, tk=tk, per_trip=per_trip),
        grid=(batch, pairs),
        in_specs=[
            pl.BlockSpec((None, seq, LANES), lambda b, p: (b, 0, p)),
            pl.BlockSpec((None, seq, LANES), lambda b, p: (b, 0, pairs + p)),
            pl.BlockSpec((None, seq, LANES), lambda b, p: (b, 0, 2 * pairs + p)),
            pl.BlockSpec((2 * tk, tk), lambda b, p: (0, 0)),
        ],
        out_specs=pl.BlockSpec((None, seq, LANES), lambda b, p: (b, 0, p)),
        out_shape=jax.ShapeDtypeStruct((batch, seq, d_model), _bf16),
        scratch_shapes=[pltpu.VMEM((tq, LANES), _f32), pltpu.VMEM((2, tq, tk), _f32)],
        compiler_params=_compiler_params(("parallel", "parallel")),
        name="stickbreak",
    )(qkv, qkv, qkv, _cumsum_matrix(tk))


def _dilated_kernel(q_ref, k_ref, v_ref, o_ref, m_ref, l_ref, acc_ref, bias_ref, *, group):
    per_class = q_ref.shape[1]
    tile_start = pl.program_id(2) * per_class
    head_lanes = _head_lanes()
    scale = jnp.asarray(HEAD_DIM ** -0.5, _bf16)
    row = lax.broadcasted_iota(jnp.int32, (BLOCK, 2 * BLOCK), 0)
    col = lax.broadcasted_iota(jnp.int32, (BLOCK, 2 * BLOCK), 1)
    in_prev = col < BLOCK
    ones_by_head = ((lax.broadcasted_iota(jnp.int32, (4 * BLOCK, LANES), 0) < 2 * BLOCK)
                    == (lax.broadcasted_iota(jnp.int32, (4 * BLOCK, LANES), 1) < HEAD_DIM)
                    ).astype(_f32).astype(_bf16)
    stages = []

    for branch, (window, r) in enumerate(DILATED_BRANCHES):
        assert window // r == BLOCK
        n_sub = N_CLASSES // r
        chunk = BLOCK // n_sub
        blocks_per_class = per_class // chunk
        sub_index = lambda a: n_sub * (a % chunk) + a // chunk
        q_sub, k_sub = sub_index(row), sub_index(col % BLOCK)
        band_prev = in_prev & (k_sub >= q_sub)
        band_cur = (~in_prev) & (k_sub <= q_sub)
        bias_ref[branch, 0] = jnp.where(band_cur, 0.0, MASK_VALUE)
        bias_ref[branch, 1] = jnp.where(band_prev | band_cur, 0.0, MASK_VALUE)

        def score_stage(g, r=r, branch=branch, n_sub=n_sub, chunk=chunk, blocks_per_class=blocks_per_class):
            places, vs, biases, logits = [], [], [], {}
            for u in range(group):
                idx = g * group + u
                c_r = idx // blocks_per_class
                local = (idx % blocks_per_class) * chunk
                cur = tile_start + local
                prev = jnp.maximum(cur - chunk, 0)
                classes = [c_r + r * m for m in range(n_sub)]

                def rows(ref, start):
                    start = start if isinstance(start, int) else pl.multiple_of(start, chunk)
                    return jnp.concatenate([ref[c, pl.ds(start, chunk), :] for c in classes], axis=0)

                q = rows(q_ref, local).astype(_bf16) * scale
                k = jnp.concatenate([rows(k_ref, prev), rows(k_ref, cur)], axis=0).astype(_bf16)
                v = jnp.concatenate([rows(v_ref, prev), rows(v_ref, cur)], axis=0).astype(_bf16)
                places.append((classes, local))
                vs.append(jnp.concatenate([_by_head(v, head_lanes), ones_by_head], axis=1))
                biases.append(bias_ref[branch, 1] if local >= chunk
                              else bias_ref[branch, (cur >= chunk).astype(jnp.int32)])
                for h in range(2):
                    q_head = jnp.where(head_lanes[h], q, jnp.zeros_like(q))
                    logits[u, h] = lax.dot_general(q_head, k, (((1,), (1,)), ((), ())),
                                                   preferred_element_type=_f32)
            return branch, chunk, places, vs, biases, logits

        stages += [functools.partial(score_stage, g) for g in range(r * blocks_per_class // group)]

    def finish_stage(branch, chunk, places, vs, biases, logits):
        probs, maxes = {}, {}
        for u in range(group):
            for h in range(2):
                z = logits[u, h] + biases[u]
                maxes[u, h] = jnp.max(z, axis=1, keepdims=True)
                probs[u, h] = jnp.exp(z - maxes[u, h]).astype(_bf16)
        for u in range(group):
            classes, local = places[u]
            pv_l = jnp.dot(jnp.concatenate([probs[u, 0], probs[u, 1]], axis=1), vs[u],
                           preferred_element_type=_f32)
            pv_t, l_t = pv_l[:, :LANES], pv_l[:, LANES:]
            m_t = jnp.where(head_lanes[0], maxes[u, 0], maxes[u, 1])
            for m, c in enumerate(classes):
                part = slice(m * chunk, (m + 1) * chunk)
                place = (c, pl.ds(local, chunk), slice(None))
                if branch == 0:
                    m_ref[place] = m_t[part]
                    l_ref[place] = l_t[part]
                    acc_ref[place] = pv_t[part]
                else:
                    m_old = m_ref[place]
                    m_new = jnp.maximum(m_old, m_t[part])
                    a_old = jnp.exp(m_old - m_new)
                    a_new = jnp.exp(m_t[part] - m_new)
                    m_ref[place] = m_new
                    l_ref[place] = a_old * l_ref[place] + a_new * l_t[part]
                    acc_ref[place] = a_old * acc_ref[place] + a_new * pv_t[part]

    scores = stages[0]()
    for nxt in stages[1:] + [None]:
        ahead = nxt() if nxt is not None else None
        finish_stage(*scores)
        scores = ahead

    o_ref[...] = (acc_ref[...] / l_ref[...]).astype(o_ref.dtype)


def _dilated(q, kv, *, batch, seq, d_model, group=4):
    pairs = d_model // LANES
    per_class = seq // N_CLASSES
    assert per_class % BLOCK == 0 and N_CLASSES % group == 0
    return pl.pallas_call(
        functools.partial(_dilated_kernel, group=group),
        grid=(batch, pairs, per_class // BLOCK),
        in_specs=[
            pl.BlockSpec((None, N_CLASSES, BLOCK, LANES), lambda b, p, i: (b, 0, i, p)),
            pl.BlockSpec((None, N_CLASSES, per_class, LANES), lambda b, p, i: (b, 0, 0, p)),
            pl.BlockSpec((None, N_CLASSES, per_class, LANES), lambda b, p, i: (b, 0, 0, pairs + p)),
        ],
        out_specs=pl.BlockSpec((None, N_CLASSES, BLOCK, LANES), lambda b, p, i: (b, 0, i, p)),
        out_shape=jax.ShapeDtypeStruct((batch, N_CLASSES, per_class, d_model), _bf16),
        scratch_shapes=[pltpu.VMEM((N_CLASSES, BLOCK, LANES), _f32)] * 3
        + [pltpu.VMEM((len(DILATED_BRANCHES), 2, BLOCK, 2 * BLOCK), _f32)],
        compiler_params=_compiler_params(("parallel", "parallel", "arbitrary")),
        name="dilated",
    )(q, kv, kv)


def _post_attn_kernel(h_ref, o_ref, wo_ref, g_ref, wg_ref, wu_ref, wd_ref, gf_ref, out_ref, acc_ref,
                      *, tf, final_norm):
    h1 = h_ref[...] + jnp.dot(o_ref[...], wo_ref[...], preferred_element_type=_f32)
    xn = _rms(h1, g_ref[...]).astype(_bf16)
    acc_ref[...] = h1
    for c in range(wg_ref.shape[1] // tf):
        cols = slice(c * tf, (c + 1) * tf)
        gate = jnp.dot(xn, wg_ref[:, cols], preferred_element_type=_f32)
        up = jnp.dot(xn, wu_ref[:, cols], preferred_element_type=_f32)
        act = (gate / (1.0 + jnp.exp(-gate)) * up).astype(_bf16)
        acc_ref[...] += jnp.dot(act, wd_ref[cols, :], preferred_element_type=_f32)
    out = acc_ref[...]
    if final_norm:
        out = _rms(out, gf_ref[...])
    out_ref[...] = out


def _post_attn(h, o, wo, g, wg, wu, wd, g_final, *, final_norm, tm=512, tf=256):
    m, d = h.shape
    f = wg.shape[1]
    resident = lambda shape: pl.BlockSpec(shape, lambda i: (0, 0), pipeline_mode=pl.Buffered(1))
    return pl.pallas_call(
        functools.partial(_post_attn_kernel, tf=tf, final_norm=final_norm),
        grid=(m // tm,),
        in_specs=[
            pl.BlockSpec((tm, d), lambda i: (i, 0)),
            pl.BlockSpec((tm, d), lambda i: (i, 0)),
            resident((d, d)),
            resident((1, d)),
            resident((d, f)),
            resident((d, f)),
            resident((f, d)),
            resident((1, d)),
        ],
        out_specs=pl.BlockSpec((tm, d), lambda i: (i, 0)),
        out_shape=jax.ShapeDtypeStruct((m, d), _f32),
        scratch_shapes=[pltpu.VMEM((tm, d), _f32)],
        compiler_params=_compiler_params(("parallel",)),
        name="post_attn",
    )(h, o, wo, g, wg, wu, wd, g_final)


def _rope_tables(seq):
    inv_freq = ROPE_THETA ** (-jnp.arange(0, HEAD_DIM, 2, dtype=_f32) / HEAD_DIM)
    ang = jnp.arange(seq, dtype=_f32)[:, None] * inv_freq[None, :]
    cos, sin = jnp.cos(ang), jnp.sin(ang)
    reps = LANES // HEAD_DIM
    return (jnp.tile(jnp.concatenate([cos, cos], axis=1), (1, reps)),
            jnp.tile(jnp.concatenate([-sin, sin], axis=1), (1, reps)))


def _class_major(a, lead):
    s, f = a.shape[-2:]
    return jnp.swapaxes(a.reshape(*lead, s // N_CLASSES, N_CLASSES, f), -3, -2)


def kernel(x, norm_mix, w_qkv_a, w_o_a, norm_kv, w_kv, w_q_b, w_o_b, norm_ffn, w_gate, w_up, w_down,
           norm_final):
    b, s, d = x.shape
    depth = norm_mix.shape[0]
    n_a = w_qkv_a.shape[0]
    assert d % LANES == 0 and s % (N_CLASSES * BLOCK) == 0
    cos, sin = _rope_tables(s)
    cos_cm, sin_cm = (_class_major(t, ()).reshape(s, LANES) for t in (cos, sin))
    cast = lambda w: w.astype(_bf16)
    w_qkv_a, w_o_a, w_kv, w_q_b, w_o_b = map(cast, (w_qkv_a, w_o_a, w_kv, w_q_b, w_o_b))
    w_gate, w_up, w_down = map(cast, (w_gate, w_up, w_down))
    g_final = norm_final.reshape(1, d)
    per_class = s // N_CLASSES

    h = x.reshape(b * s, d)
    kv = None
    for layer in range(depth):
        g_mix = norm_mix[layer].reshape(1, d)
        if layer < n_a:
            qkv = _norm_proj(h, g_mix, w_qkv_a[layer], cos, sin, rope_cols=0, out_dtype=_bf16, seq=s)
            o = _stickbreak(qkv.reshape(b, s, 3 * d), batch=b, seq=s, d_model=d)
            w_o = w_o_a[layer]
        else:
            j = layer - n_a
            if kv is None:
                h = _class_major(h.reshape(b, s, d), (b,)).reshape(b * s, d)
                kv = _norm_proj(h, norm_kv.reshape(1, d), w_kv, cos_cm, sin_cm, rope_cols=d, out_dtype=_f32,
                                seq=s).reshape(b, N_CLASSES, per_class, 2 * d)
            q = _norm_proj(h, g_mix, w_q_b[j], cos_cm, sin_cm, rope_cols=d, out_dtype=_f32, seq=s)
            o = _dilated(q.reshape(b, N_CLASSES, per_class, d), kv, batch=b, seq=s, d_model=d)
            w_o = w_o_b[j]
        h = _post_attn(h, o.reshape(b * s, d), w_o, norm_ffn[layer].reshape(1, d), w_gate[layer],
                       w_up[layer], w_down[layer], g_final, final_norm=(layer == depth - 1))
    if kv is not None:
        h = jnp.swapaxes(h.reshape(b, N_CLASSES, per_class, d), 1, 2)
    return h.reshape(b, s, d)
```

```python
import functools

import jax
import jax.numpy as jnp
from jax import lax
from jax.experimental import pallas as pl
from jax.experimental.pallas import tpu as pltpu

HEAD_DIM = 64
BLOCK = 128
ROPE_THETA = 10000.0
NORM_EPS = 1e-6
DILATED_BRANCHES = ((128, 1), (512, 4), (2048, 16))
N_CLASSES = max(r for _, r in DILATED_BRANCHES)

LANES = 128
VMEM_LIMIT_BYTES = 56 * 1024 * 1024
MASK_VALUE = -1e30
EXP2_UNDERFLOW = -160.0
LOG2_E = 1.4426950408889634
NEAR_KEYS = 176
MXU_WIDTH = 256

_f32 = jnp.float32
_bf16 = jnp.bfloat16


def _rms(x, g):
    return x * lax.rsqrt(jnp.mean(x * x, axis=-1, keepdims=True) + NORM_EPS) * g


def _compiler_params(semantics):
    return pltpu.CompilerParams(dimension_semantics=semantics, vmem_limit_bytes=VMEM_LIMIT_BYTES)


def _head_lanes():
    lane = lax.broadcasted_iota(jnp.int32, (1, LANES), 1)
    return lane < HEAD_DIM, lane >= HEAD_DIM


def _by_head(x, head_lanes):
    return jnp.concatenate([jnp.where(hm, x, jnp.zeros_like(x)) for hm in head_lanes], axis=0)


def _norm_proj_kernel(x_ref, g_ref, w_ref, cos_ref, sin_ref, o_ref, *, rope_cols):
    xn = _rms(x_ref[...], g_ref[...]).astype(_bf16)
    n_out = o_ref.shape[1]
    tn = MXU_WIDTH
    if rope_cols:
        reps = tn // LANES
        cos = jnp.concatenate([cos_ref[...]] * reps, axis=1)
        sin = jnp.concatenate([sin_ref[...]] * reps, axis=1)
        first_half = (lax.broadcasted_iota(jnp.int32, cos.shape, 1) % HEAD_DIM) < HEAD_DIM // 2
    for c in range(n_out // tn):
        cols = slice(c * tn, (c + 1) * tn)
        y = jnp.dot(xn, w_ref[:, cols], preferred_element_type=_f32)
        if c * tn < rope_cols:
            partner = jnp.where(first_half, pltpu.roll(y, tn - HEAD_DIM // 2, 1),
                                pltpu.roll(y, HEAD_DIM // 2, 1))
            y = y * cos + partner * sin
        o_ref[:, cols] = y.astype(o_ref.dtype)


def _norm_proj(x, g, w, cos, sin, *, rope_cols, out_dtype, seq, tm=512):
    m, d = x.shape
    n_out = w.shape[1]
    pos_blocks = seq // tm
    return pl.pallas_call(
        functools.partial(_norm_proj_kernel, rope_cols=rope_cols),
        grid=(m // tm,),
        in_specs=[
            pl.BlockSpec((tm, d), lambda i: (i, 0)),
            pl.BlockSpec((1, d), lambda i: (0, 0)),
            pl.BlockSpec((d, n_out), lambda i: (0, 0)),
            pl.BlockSpec((tm, LANES), lambda i: (i % pos_blocks, 0)),
            pl.BlockSpec((tm, LANES), lambda i: (i % pos_blocks, 0)),
        ],
        out_specs=pl.BlockSpec((tm, n_out), lambda i: (i, 0)),
        out_shape=jax.ShapeDtypeStruct((m, n_out), out_dtype),
        compiler_params=_compiler_params(("parallel",)),
        name="norm_proj",
    )(x, g, w, cos, sin)


def _stickbreak_kernel(q_ref, k_ref, v_ref, uu_ref, o_ref, acc_ref, carry_ref, c_max_ref,
                       *, tq, tk, per_trip):
    def start_tile(qi, slot=0, first_tile=False, older=False):
        rows = pl.ds(qi * tq if first_tile else pl.multiple_of(qi * tq, tq), tq)
        return _stickbreak_tile(qi, q_ref.at[rows], k_ref, v_ref, uu_ref, o_ref.at[rows], acc_ref, carry_ref,
                                c_max_ref, tq=tq, tk=tk, slot=slot, first_tile=first_tile, older=older)

    def tiles(trip, carry):
        first = 1 + trip * per_trip
        finish = start_tile(first, 0)
        for j in range(per_trip):
            ahead = start_tile(first + j + 1, j + 1) if j + 1 < per_trip else None
            finish()
            finish = ahead

        def older_keys(j, c):
            @pl.when(c_max_ref[j] > EXP2_UNDERFLOW)
            def _():
                start_tile(first + j, j, older=True)
            return c

        lax.fori_loop(0, per_trip, older_keys, 0)
        return carry

    n_tiles = q_ref.shape[0] // tq
    assert (n_tiles - 1) % per_trip == 0
    start_tile(0, first_tile=True)()
    lax.fori_loop(0, (n_tiles - 1) // per_trip, tiles, 0)


def _stickbreak_tile(qi, q_ref, k_ref, v_ref, uu_ref, o_ref, acc_ref, carry_ref, c_max_ref,
                     *, tq, tk, slot, first_tile, older):
    head_lanes = _head_lanes()
    q = q_ref[...] * jnp.asarray(HEAD_DIM ** -0.5, _bf16)
    uu = uu_ref[...]
    strict = (lax.broadcasted_iota(jnp.int32, (tk, tk), 1)
              < lax.broadcasted_iota(jnp.int32, (tk, tk), 0))
    sign_bit = jnp.uint32(0x80000000)

    def mask_top(a, diagonal):
        if not diagonal:
            return a
        top = jnp.where(strict, a[:tk], 0.0)
        return top if a.shape[0] == tk else jnp.concatenate([top, a[tk:]], axis=0)

    def add_rows(x, r0, r1, delta):
        parts = ([x[:r0]] if r0 else []) + [x[r0:r1] + delta] + ([x[r1:]] if r1 < tq else [])
        return parts[0] if len(parts) == 1 else jnp.concatenate(parts, axis=0)

    def score_blocks(blocks):
        values, logits, afters = [], {}, {}
        for b, (jb, r0, r1, _) in enumerate(blocks):
            start = pl.multiple_of(jb * tk, tk)
            z = lax.dot_general(q[r0:r1], _by_head(k_ref[pl.ds(start, tk), :], head_lanes),
                                (((1,), (1,)), ((), ())), preferred_element_type=_f32)
            logits[b, 0], logits[b, 1] = z[:, :tk] * LOG2_E, z[:, tk:] * LOG2_E
            values.append(_by_head(v_ref[pl.ds(start, tk), :], head_lanes))
        for b, (_, _, _, diagonal) in enumerate(blocks):
            for h in range(2):
                z = logits[b, h]
                neg_abs = lax.bitcast_convert_type(lax.bitcast_convert_type(z, jnp.uint32) | sign_bit, _f32)
                softplus = jnp.maximum(z, 0.0) + jnp.log2(1.0 + jnp.exp2(neg_abs))
                sp = mask_top(softplus, diagonal)
                hi = sp.astype(_bf16)
                lo = (sp - hi.astype(_f32)).astype(_bf16)
                afters[b, h] = jnp.dot(jnp.concatenate([hi, lo], axis=1), uu,
                                       preferred_element_type=_f32)
        return values, logits, afters

    def weigh_blocks(blocks, scores, state):
        values, logits, afters = scores
        carries, acc = list(state[:2]), state[2]
        for b, (_, r0, r1, diagonal) in enumerate(blocks):
            weights = []
            for h in range(2):
                after = afters[b, h]
                w = mask_top(jnp.exp2(logits[b, h] + after + carries[h][r0:r1]), diagonal)
                weights.append(w.astype(_bf16))
                carries[h] = add_rows(carries[h], r0, r1, jnp.broadcast_to(after[:, :1], after.shape))
            pv = jnp.dot(jnp.concatenate(weights, axis=1), values[b], preferred_element_type=_f32)
            acc = add_rows(acc, r0, r1, pv)
        return carries[0], carries[1], acc

    def key_blocks(blocks, state):
        return weigh_blocks(blocks, score_blocks(blocks), state)

    blocks_per_tile = tq // tk
    first_diag = qi * blocks_per_tile

    skipped_from = [min(tq, max(NEAR_KEYS - d * tk, 0)) for d in range(blocks_per_tile)]
    partly_skipped = [d for d in range(blocks_per_tile) if skipped_from[d] < tq]

    near_blocks = [(first_diag + d, d * tk, tq, True) for d in reversed(range(blocks_per_tile))]
    if not first_tile:
        near_blocks += [(first_diag - 1 - d, 0, skipped_from[d], False)
                        for d in range(blocks_per_tile) if skipped_from[d]]
    def carry_max(state):
        return jnp.max(jnp.maximum(state[0], state[1]))

    def save(state):
        carry_ref[slot, 0], carry_ref[slot, 1], acc_ref[slot] = state
        return state

    def load():
        return carry_ref[slot, 0], carry_ref[slot, 1], acc_ref[slot]

    def more(loop):
        t, c_max = loop
        return (t < qi) & (c_max > EXP2_UNDERFLOW)

    def older_keys(loop):
        t, _ = loop
        blocks = [(first_diag - 1 - t * blocks_per_tile - d, 0, tq, False) for d in range(blocks_per_tile)]
        return t + 1, carry_max(save(key_blocks(blocks, load())))

    if older:
        for d in partly_skipped:
            save(key_blocks([(first_diag - 1 - d, skipped_from[d], tq, False)], load()))
        lax.while_loop(more, older_keys, (jnp.int32(1), c_max_ref[slot]))
        o_ref[...] = acc_ref[slot].astype(o_ref.dtype)
        return None

    near_scores = score_blocks(near_blocks)

    def finish():
        zero = jnp.zeros((tq, LANES), _f32)
        state = weigh_blocks(near_blocks, near_scores, (zero, zero, zero))
        o_ref[...] = state[2].astype(o_ref.dtype)
        if not first_tile:
            c_max_ref[slot] = carry_max(save(state))

    return finish


def _cumsum_matrix(tk):
    src = jnp.arange(2 * tk)[:, None] % tk
    dst = jnp.arange(tk)[None, :]
    return -(src >= dst).astype(_bf16)


def _stickbreak(qkv, *, batch, seq, d_model, tq=256, tk=128, per_trip=5):
    pairs = d_model // LANES
    return pl.pallas_call(
        functools.partial(_stickbreak_kernel, tq=tq, tk=tk, per_trip=per_trip),
        grid=(batch, pairs),
        in_specs=[
            pl.BlockSpec((None, seq, LANES), lambda b, p: (b, 0, p)),
            pl.BlockSpec((None, seq, LANES), lambda b, p: (b, 0, pairs + p)),
            pl.BlockSpec((None, seq, LANES), lambda b, p: (b, 0, 2 * pairs + p)),
            pl.BlockSpec((2 * tk, tk), lambda b, p: (0, 0)),
        ],
        out_specs=pl.BlockSpec((None, seq, LANES), lambda b, p: (b, 0, p)),
        out_shape=jax.ShapeDtypeStruct((batch, seq, d_model), _bf16),
        scratch_shapes=[pltpu.VMEM((per_trip, tq, LANES), _f32), pltpu.VMEM((per_trip, 2, tq, tk), _f32),
                        pltpu.SMEM((per_trip,), _f32)],
        compiler_params=_compiler_params(("parallel", "parallel")),
        name="stickbreak",
    )(qkv, qkv, qkv, _cumsum_matrix(tk))


def _dilated_kernel(q_ref, k_ref, v_ref, o_ref, m_ref, l_ref, acc_ref, bias_ref, *, group):
    per_class = q_ref.shape[1]
    tile_start = pl.program_id(2) * per_class
    head_lanes = _head_lanes()
    scale = jnp.asarray(HEAD_DIM ** -0.5, _bf16)
    row = lax.broadcasted_iota(jnp.int32, (BLOCK, 2 * BLOCK), 0)
    col = lax.broadcasted_iota(jnp.int32, (BLOCK, 2 * BLOCK), 1)
    in_prev = col < BLOCK
    ones_by_head = ((lax.broadcasted_iota(jnp.int32, (4 * BLOCK, LANES), 0) < 2 * BLOCK)
                    == (lax.broadcasted_iota(jnp.int32, (4 * BLOCK, LANES), 1) < HEAD_DIM)
                    ).astype(_f32).astype(_bf16)
    stages = []

    for branch, (window, r) in enumerate(DILATED_BRANCHES):
        assert window // r == BLOCK
        n_sub = N_CLASSES // r
        chunk = BLOCK // n_sub
        blocks_per_class = per_class // chunk
        sub_index = lambda a: n_sub * (a % chunk) + a // chunk
        q_sub, k_sub = sub_index(row), sub_index(col % BLOCK)
        band_prev = in_prev & (k_sub >= q_sub)
        band_cur = (~in_prev) & (k_sub <= q_sub)
        bias_ref[branch, 0] = jnp.where(band_cur, 0.0, MASK_VALUE)
        bias_ref[branch, 1] = jnp.where(band_prev | band_cur, 0.0, MASK_VALUE)

        def score_stage(g, r=r, branch=branch, n_sub=n_sub, chunk=chunk, blocks_per_class=blocks_per_class):
            places, vs, biases, logits = [], [], [], {}
            for u in range(group):
                idx = g * group + u
                c_r = idx // blocks_per_class
                local = (idx % blocks_per_class) * chunk
                cur = tile_start + local
                prev = jnp.maximum(cur - chunk, 0)
                classes = [c_r + r * m for m in range(n_sub)]

                def rows(ref, start):
                    start = start if isinstance(start, int) else pl.multiple_of(start, chunk)
                    return jnp.concatenate([ref[c, pl.ds(start, chunk), :] for c in classes], axis=0)

                q = rows(q_ref, local).astype(_bf16) * scale
                k = jnp.concatenate([rows(k_ref, prev), rows(k_ref, cur)], axis=0).astype(_bf16)
                v = jnp.concatenate([rows(v_ref, prev), rows(v_ref, cur)], axis=0).astype(_bf16)
                places.append((classes, local))
                vs.append(jnp.concatenate([_by_head(v, head_lanes), ones_by_head], axis=1))
                biases.append(bias_ref[branch, 1] if local >= chunk
                              else bias_ref[branch, (cur >= chunk).astype(jnp.int32)])
                for h in range(2):
                    q_head = jnp.where(head_lanes[h], q, jnp.zeros_like(q))
                    logits[u, h] = lax.dot_general(q_head, k, (((1,), (1,)), ((), ())),
                                                   preferred_element_type=_f32)
            return branch, chunk, places, vs, biases, logits

        stages += [functools.partial(score_stage, g) for g in range(r * blocks_per_class // group)]

    def finish_stage(branch, chunk, places, vs, biases, logits):
        probs, maxes = {}, {}
        for u in range(group):
            for h in range(2):
                z = logits[u, h] + biases[u]
                maxes[u, h] = jnp.max(z, axis=1, keepdims=True)
                probs[u, h] = jnp.exp(z - maxes[u, h]).astype(_bf16)
        for u in range(group):
            classes, local = places[u]
            pv_l = jnp.dot(jnp.concatenate([probs[u, 0], probs[u, 1]], axis=1), vs[u],
                           preferred_element_type=_f32)
            pv_t, l_t = pv_l[:, :LANES], pv_l[:, LANES:]
            m_t = jnp.where(head_lanes[0], maxes[u, 0], maxes[u, 1])
            for m, c in enumerate(classes):
                part = slice(m * chunk, (m + 1) * chunk)
                place = (c, pl.ds(local, chunk), slice(None))
                if branch == 0:
                    m_ref[place] = m_t[part]
                    l_ref[place] = l_t[part]
                    acc_ref[place] = pv_t[part]
                else:
                    m_old = m_ref[place]
                    m_new = jnp.maximum(m_old, m_t[part])
                    a_old = jnp.exp(m_old - m_new)
                    a_new = jnp.exp(m_t[part] - m_new)
                    m_ref[place] = m_new
                    l_ref[place] = a_old * l_ref[place] + a_new * l_t[part]
                    acc_ref[place] = a_old * acc_ref[place] + a_new * pv_t[part]

    scores = stages[0]()
    for nxt in stages[1:] + [None]:
        ahead = nxt() if nxt is not None else None
        finish_stage(*scores)
        scores = ahead

    o_ref[...] = (acc_ref[...] / l_ref[...]).astype(o_ref.dtype)


def _dilated(q, kv, *, batch, seq, d_model, group=4):
    pairs = d_model // LANES
    per_class = seq // N_CLASSES
    assert per_class % BLOCK == 0 and N_CLASSES % group == 0
    return pl.pallas_call(
        functools.partial(_dilated_kernel, group=group),
        grid=(batch, pairs, per_class // BLOCK),
        in_specs=[
            pl.BlockSpec((None, N_CLASSES, BLOCK, LANES), lambda b, p, i: (b, 0, i, p)),
            pl.BlockSpec((None, N_CLASSES, per_class, LANES), lambda b, p, i: (b, 0, 0, p)),
            pl.BlockSpec((None, N_CLASSES, per_class, LANES), lambda b, p, i: (b, 0, 0, pairs + p)),
        ],
        out_specs=pl.BlockSpec((None, N_CLASSES, BLOCK, LANES), lambda b, p, i: (b, 0, i, p)),
        out_shape=jax.ShapeDtypeStruct((batch, N_CLASSES, per_class, d_model), _bf16),
        scratch_shapes=[pltpu.VMEM((N_CLASSES, BLOCK, LANES), _f32)] * 3
        + [pltpu.VMEM((len(DILATED_BRANCHES), 2, BLOCK, 2 * BLOCK), _f32)],
        compiler_params=_compiler_params(("parallel", "parallel", "arbitrary")),
        name="dilated",
    )(q, kv, kv)


def _post_attn_kernel(h_ref, o_ref, wo_ref, g_ref, wg_ref, wu_ref, wd_ref, gf_ref, out_ref, acc_ref,
                      *, tf, final_norm):
    h1 = h_ref[...] + jnp.dot(o_ref[...], wo_ref[...], preferred_element_type=_f32)
    xn = _rms(h1, g_ref[...]).astype(_bf16)
    acc_ref[...] = h1
    for c in range(wg_ref.shape[1] // tf):
        cols = slice(c * tf, (c + 1) * tf)
        gate = jnp.dot(xn, wg_ref[:, cols], preferred_element_type=_f32)
        up = jnp.dot(xn, wu_ref[:, cols], preferred_element_type=_f32)
        act = (gate / (1.0 + jnp.exp(-gate)) * up).astype(_bf16)
        acc_ref[...] += jnp.dot(act, wd_ref[cols, :], preferred_element_type=_f32)
    out = acc_ref[...]
    if final_norm:
        out = _rms(out, gf_ref[...])
    out_ref[...] = out


def _post_attn(h, o, wo, g, wg, wu, wd, g_final, *, final_norm, tm=512, tf=256):
    m, d = h.shape
    f = wg.shape[1]
    resident = lambda shape: pl.BlockSpec(shape, lambda i: (0, 0), pipeline_mode=pl.Buffered(1))
    return pl.pallas_call(
        functools.partial(_post_attn_kernel, tf=tf, final_norm=final_norm),
        grid=(m // tm,),
        in_specs=[
            pl.BlockSpec((tm, d), lambda i: (i, 0)),
            pl.BlockSpec((tm, d), lambda i: (i, 0)),
            resident((d, d)),
            resident((1, d)),
            resident((d, f)),
            resident((d, f)),
            resident((f, d)),
            resident((1, d)),
        ],
        out_specs=pl.BlockSpec((tm, d), lambda i: (i, 0)),
        out_shape=jax.ShapeDtypeStruct((m, d), _f32),
        scratch_shapes=[pltpu.VMEM((tm, d), _f32)],
        compiler_params=_compiler_params(("parallel",)),
        name="post_attn",
    )(h, o, wo, g, wg, wu, wd, g_final)


def _rope_tables(seq):
    inv_freq = ROPE_THETA ** (-jnp.arange(0, HEAD_DIM, 2, dtype=_f32) / HEAD_DIM)
    ang = jnp.arange(seq, dtype=_f32)[:, None] * inv_freq[None, :]
    cos, sin = jnp.cos(ang), jnp.sin(ang)
    reps = LANES // HEAD_DIM
    return (jnp.tile(jnp.concatenate([cos, cos], axis=1), (1, reps)),
            jnp.tile(jnp.concatenate([-sin, sin], axis=1), (1, reps)))


def _class_major(a, lead):
    s, f = a.shape[-2:]
    return jnp.swapaxes(a.reshape(*lead, s // N_CLASSES, N_CLASSES, f), -3, -2)


def kernel(x, norm_mix, w_qkv_a, w_o_a, norm_kv, w_kv, w_q_b, w_o_b, norm_ffn, w_gate, w_up, w_down,
           norm_final):
    b, s, d = x.shape
    depth = norm_mix.shape[0]
    n_a = w_qkv_a.shape[0]
    assert d % LANES == 0 and s % (N_CLASSES * BLOCK) == 0
    cos, sin = _rope_tables(s)
    cos_cm, sin_cm = (_class_major(t, ()).reshape(s, LANES) for t in (cos, sin))
    cast = lambda w: w.astype(_bf16)
    w_qkv_a, w_o_a, w_kv, w_q_b, w_o_b = map(cast, (w_qkv_a, w_o_a, w_kv, w_q_b, w_o_b))
    w_gate, w_up, w_down = map(cast, (w_gate, w_up, w_down))
    g_final = norm_final.reshape(1, d)
    per_class = s // N_CLASSES

    h = x.reshape(b * s, d)
    kv = None
    for layer in range(depth):
        g_mix = norm_mix[layer].reshape(1, d)
        if layer < n_a:
            qkv = _norm_proj(h, g_mix, w_qkv_a[layer], cos, sin, rope_cols=0, out_dtype=_bf16, seq=s)
            o = _stickbreak(qkv.reshape(b, s, 3 * d), batch=b, seq=s, d_model=d)
            w_o = w_o_a[layer]
        else:
            j = layer - n_a
            if kv is None:
                h = _class_major(h.reshape(b, s, d), (b,)).reshape(b * s, d)
                kv = _norm_proj(h, norm_kv.reshape(1, d), w_kv, cos_cm, sin_cm, rope_cols=d, out_dtype=_f32,
                                seq=s).reshape(b, N_CLASSES, per_class, 2 * d)
            q = _norm_proj(h, g_mix, w_q_b[j], cos_cm, sin_cm, rope_cols=d, out_dtype=_f32, seq=s)
            o = _dilated(q.reshape(b, N_CLASSES, per_class, d), kv, batch=b, seq=s, d_model=d)
            w_o = w_o_b[j]
        h = _post_attn(h, o.reshape(b * s, d), w_o, norm_ffn[layer].reshape(1, d), w_gate[layer],
                       w_up[layer], w_down[layer], g_final, final_norm=(layer == depth - 1))
    if kv is not None:
        h = jnp.swapaxes(h.reshape(b, N_CLASSES, per_class, d), 1, 2)
    return h.reshape(b, s, d)
```

```python
import functools

import jax
import jax.numpy as jnp
from jax import lax
from jax.experimental import pallas as pl
from jax.experimental.pallas import tpu as pltpu

HEAD_DIM = 64
BLOCK = 128
ROPE_THETA = 10000.0
NORM_EPS = 1e-6
DILATED_BRANCHES = ((128, 1), (512, 4), (2048, 16))
N_CLASSES = max(r for _, r in DILATED_BRANCHES)

LANES = 128
VMEM_LIMIT_BYTES = 56 * 1024 * 1024
MASK_VALUE = -1e30
EXP2_UNDERFLOW = -160.0
LOG2_E = 1.4426950408889634
NEAR_KEYS = 176
MXU_WIDTH = 256

_f32 = jnp.float32
_bf16 = jnp.bfloat16


def _rms(x, g):
    return x * lax.rsqrt(jnp.mean(x * x, axis=-1, keepdims=True) + NORM_EPS) * g


def _compiler_params(semantics):
    return pltpu.CompilerParams(dimension_semantics=semantics, vmem_limit_bytes=VMEM_LIMIT_BYTES)


def _head_lanes():
    lane = lax.broadcasted_iota(jnp.int32, (1, LANES), 1)
    return lane < HEAD_DIM, lane >= HEAD_DIM


def _by_head(x, head_lanes):
    return jnp.concatenate([jnp.where(hm, x, jnp.zeros_like(x)) for hm in head_lanes], axis=0)


def _norm_proj_kernel(x_ref, g_ref, w_ref, cos_ref, sin_ref, o_ref, *, rope_cols):
    xn = _rms(x_ref[...], g_ref[...]).astype(_bf16)
    n_out = o_ref.shape[1]
    tn = MXU_WIDTH
    if rope_cols:
        reps = tn // LANES
        cos = jnp.concatenate([cos_ref[...]] * reps, axis=1)
        sin = jnp.concatenate([sin_ref[...]] * reps, axis=1)
        first_half = (lax.broadcasted_iota(jnp.int32, cos.shape, 1) % HEAD_DIM) < HEAD_DIM // 2
    for c in range(n_out // tn):
        cols = slice(c * tn, (c + 1) * tn)
        y = jnp.dot(xn, w_ref[:, cols], preferred_element_type=_f32)
        if c * tn < rope_cols:
            partner = jnp.where(first_half, pltpu.roll(y, tn - HEAD_DIM // 2, 1),
                                pltpu.roll(y, HEAD_DIM // 2, 1))
            y = y * cos + partner * sin
        o_ref[:, cols] = y.astype(o_ref.dtype)


def _norm_proj(x, g, w, cos, sin, *, rope_cols, out_dtype, seq, tm=512):
    m, d = x.shape
    n_out = w.shape[1]
    pos_blocks = seq // tm
    return pl.pallas_call(
        functools.partial(_norm_proj_kernel, rope_cols=rope_cols),
        grid=(m // tm,),
        in_specs=[
            pl.BlockSpec((tm, d), lambda i: (i, 0)),
            pl.BlockSpec((1, d), lambda i: (0, 0)),
            pl.BlockSpec((d, n_out), lambda i: (0, 0)),
            pl.BlockSpec((tm, LANES), lambda i: (i % pos_blocks, 0)),
            pl.BlockSpec((tm, LANES), lambda i: (i % pos_blocks, 0)),
        ],
        out_specs=pl.BlockSpec((tm, n_out), lambda i: (i, 0)),
        out_shape=jax.ShapeDtypeStruct((m, n_out), out_dtype),
        compiler_params=_compiler_params(("parallel",)),
        name="norm_proj",
    )(x, g, w, cos, sin)


def _stickbreak_kernel(q_ref, k_ref, v_ref, uu_ref, o_ref, acc_ref, carry_ref, c_max_ref,
                       *, tq, tk, per_trip):
    def start_tile(qi, slot=0, first_tile=False, older=False):
        rows = pl.ds(qi * tq if first_tile else pl.multiple_of(qi * tq, tq), tq)
        return _stickbreak_tile(qi, q_ref.at[rows], k_ref, v_ref, uu_ref, o_ref.at[rows], acc_ref, carry_ref,
                                c_max_ref, tq=tq, tk=tk, slot=slot, first_tile=first_tile, older=older)

    def tiles(trip, carry):
        first = 1 + trip * per_trip
        finish = start_tile(first, 0)
        for j in range(per_trip):
            ahead = start_tile(first + j + 1, j + 1) if j + 1 < per_trip else None
            finish()
            finish = ahead

        def older_keys(j, c):
            @pl.when(c_max_ref[j] > EXP2_UNDERFLOW)
            def _():
                start_tile(first + j, j, older=True)
            return c

        lax.fori_loop(0, per_trip, older_keys, 0)
        return carry

    n_tiles = q_ref.shape[0] // tq
    assert (n_tiles - 1) % per_trip == 0
    start_tile(0, first_tile=True)()
    lax.fori_loop(0, (n_tiles - 1) // per_trip, tiles, 0)


def _stickbreak_tile(qi, q_ref, k_ref, v_ref, uu_ref, o_ref, acc_ref, carry_ref, c_max_ref,
                     *, tq, tk, slot, first_tile, older):
    head_lanes = _head_lanes()
    q = q_ref[...] * jnp.asarray(HEAD_DIM ** -0.5, _bf16)
    uu = uu_ref[...]
    strict = (lax.broadcasted_iota(jnp.int32, (tk, tk), 1)
              < lax.broadcasted_iota(jnp.int32, (tk, tk), 0))
    sign_bit = jnp.uint32(0x80000000)

    def mask_top(a, diagonal):
        if not diagonal:
            return a
        top = jnp.where(strict, a[:tk], 0.0)
        return top if a.shape[0] == tk else jnp.concatenate([top, a[tk:]], axis=0)

    def add_rows(x, r0, r1, delta):
        parts = ([x[:r0]] if r0 else []) + [x[r0:r1] + delta] + ([x[r1:]] if r1 < tq else [])
        return parts[0] if len(parts) == 1 else jnp.concatenate(parts, axis=0)

    def score_blocks(blocks):
        values, logits, afters = [], {}, {}
        for b, (jb, r0, r1, _) in enumerate(blocks):
            start = pl.multiple_of(jb * tk, tk)
            z = lax.dot_general(q[r0:r1], _by_head(k_ref[pl.ds(start, tk), :], head_lanes),
                                (((1,), (1,)), ((), ())), preferred_element_type=_f32)
            logits[b, 0], logits[b, 1] = z[:, :tk] * LOG2_E, z[:, tk:] * LOG2_E
            values.append(_by_head(v_ref[pl.ds(start, tk), :], head_lanes))
        for b, (_, _, _, diagonal) in enumerate(blocks):
            for h in range(2):
                z = logits[b, h]
                neg_abs = lax.bitcast_convert_type(lax.bitcast_convert_type(z, jnp.uint32) | sign_bit, _f32)
                softplus = jnp.maximum(z, 0.0) + jnp.log2(1.0 + jnp.exp2(neg_abs))
                sp = mask_top(softplus, diagonal)
                hi = sp.astype(_bf16)
                lo = (sp - hi.astype(_f32)).astype(_bf16)
                afters[b, h] = jnp.dot(jnp.concatenate([hi, lo], axis=1), uu,
                                       preferred_element_type=_f32)
        return values, logits, afters

    def weigh_blocks(blocks, scores, state):
        values, logits, afters = scores
        carries, acc = list(state[:2]), state[2]
        for b, (_, r0, r1, diagonal) in enumerate(blocks):
            weights = []
            for h in range(2):
                after = afters[b, h]
                w = mask_top(jnp.exp2(logits[b, h] + after + carries[h][r0:r1]), diagonal)
                weights.append(w.astype(_bf16))
                carries[h] = add_rows(carries[h], r0, r1, jnp.broadcast_to(after[:, :1], after.shape))
            pv = jnp.dot(jnp.concatenate(weights, axis=1), values[b], preferred_element_type=_f32)
            acc = add_rows(acc, r0, r1, pv)
        return carries[0], carries[1], acc

    def key_blocks(blocks, state):
        return weigh_blocks(blocks, score_blocks(blocks), state)

    blocks_per_tile = tq // tk
    first_diag = qi * blocks_per_tile

    skipped_from = [min(tq, max(NEAR_KEYS - d * tk, 0)) for d in range(blocks_per_tile)]
    partly_skipped = [d for d in range(blocks_per_tile) if skipped_from[d] < tq]

    near_blocks = [(first_diag + d, d * tk, tq, True) for d in reversed(range(blocks_per_tile))]
    if not first_tile:
        near_blocks += [(first_diag - 1 - d, 0, skipped_from[d], False)
                        for d in range(blocks_per_tile) if skipped_from[d]]
    def carry_max(state):
        return jnp.max(jnp.maximum(state[0], state[1]))

    def save(state):
        carry_ref[slot, 0], carry_ref[slot, 1], acc_ref[slot] = state
        return state

    def load():
        return carry_ref[slot, 0], carry_ref[slot, 1], acc_ref[slot]

    def more(loop):
        t, c_max = loop
        return (t < qi) & (c_max > EXP2_UNDERFLOW)

    def older_keys(loop):
        t, _ = loop
        blocks = [(first_diag - 1 - t * blocks_per_tile - d, 0, tq, False) for d in range(blocks_per_tile)]
        return t + 1, carry_max(save(key_blocks(blocks, load())))

    if older:
        for d in partly_skipped:
            save(key_blocks([(first_diag - 1 - d, skipped_from[d], tq, False)], load()))
        lax.while_loop(more, older_keys, (jnp.int32(1), c_max_ref[slot]))
        o_ref[...] = acc_ref[slot].astype(o_ref.dtype)
        return None

    near_scores = score_blocks(near_blocks)

    def finish():
        zero = jnp.zeros((tq, LANES), _f32)
        state = weigh_blocks(near_blocks, near_scores, (zero, zero, zero))
        o_ref[...] = state[2].astype(o_ref.dtype)
        if not first_tile:
            c_max_ref[slot] = carry_max(save(state))

    return finish


def _cumsum_matrix(tk):
    src = jnp.arange(2 * tk)[:, None] % tk
    dst = jnp.arange(tk)[None, :]
    return -(src >= dst).astype(_bf16)


def _stickbreak(qkv, *, batch, seq, d_model, tq=256, tk=128, per_trip=15):
    pairs = d_model // LANES
    return pl.pallas_call(
        functools.partial(_stickbreak_kernel, tq=tq, tk=tk, per_trip=per_trip),
        grid=(batch, pairs),
        in_specs=[
            pl.BlockSpec((None, seq, LANES), lambda b, p: (b, 0, p)),
            pl.BlockSpec((None, seq, LANES), lambda b, p: (b, 0, pairs + p)),
            pl.BlockSpec((None, seq, LANES), lambda b, p: (b, 0, 2 * pairs + p)),
            pl.BlockSpec((2 * tk, tk), lambda b, p: (0, 0)),
        ],
        out_specs=pl.BlockSpec((None, seq, LANES), lambda b, p: (b, 0, p)),
        out_shape=jax.ShapeDtypeStruct((batch, seq, d_model), _bf16),
        scratch_shapes=[pltpu.VMEM((per_trip, tq, LANES), _f32), pltpu.VMEM((per_trip, 2, tq, tk), _f32),
                        pltpu.SMEM((per_trip,), _f32)],
        compiler_params=_compiler_params(("parallel", "parallel")),
        name="stickbreak",
    )(qkv, qkv, qkv, _cumsum_matrix(tk))


def _dilated_kernel(q_ref, k_ref, v_ref, o_ref, m_ref, l_ref, acc_ref, bias_ref, *, group):
    per_class = q_ref.shape[1]
    tile_start = pl.program_id(2) * per_class
    head_lanes = _head_lanes()
    scale = jnp.asarray(HEAD_DIM ** -0.5, _bf16)
    row = lax.broadcasted_iota(jnp.int32, (BLOCK, 2 * BLOCK), 0)
    col = lax.broadcasted_iota(jnp.int32, (BLOCK, 2 * BLOCK), 1)
    in_prev = col < BLOCK
    ones_by_head = ((lax.broadcasted_iota(jnp.int32, (4 * BLOCK, LANES), 0) < 2 * BLOCK)
                    == (lax.broadcasted_iota(jnp.int32, (4 * BLOCK, LANES), 1) < HEAD_DIM)
                    ).astype(_f32).astype(_bf16)
    stages = []

    for branch, (window, r) in enumerate(DILATED_BRANCHES):
        assert window // r == BLOCK
        n_sub = N_CLASSES // r
        chunk = BLOCK // n_sub
        blocks_per_class = per_class // chunk
        sub_index = lambda a: n_sub * (a % chunk) + a // chunk
        q_sub, k_sub = sub_index(row), sub_index(col % BLOCK)
        band_prev = in_prev & (k_sub >= q_sub)
        band_cur = (~in_prev) & (k_sub <= q_sub)
        bias_ref[branch, 0] = jnp.where(band_cur, 0.0, MASK_VALUE)
        bias_ref[branch, 1] = jnp.where(band_prev | band_cur, 0.0, MASK_VALUE)

        def score_stage(g, r=r, branch=branch, n_sub=n_sub, chunk=chunk, blocks_per_class=blocks_per_class):
            places, vs, biases, logits = [], [], [], {}
            for u in range(group):
                idx = g * group + u
                c_r = idx // blocks_per_class
                local = (idx % blocks_per_class) * chunk
                cur = tile_start + local
                prev = jnp.maximum(cur - chunk, 0)
                classes = [c_r + r * m for m in range(n_sub)]

                def rows(ref, start):
                    start = start if isinstance(start, int) else pl.multiple_of(start, chunk)
                    return jnp.concatenate([ref[c, pl.ds(start, chunk), :] for c in classes], axis=0)

                q = rows(q_ref, local).astype(_bf16) * scale
                k = jnp.concatenate([rows(k_ref, prev), rows(k_ref, cur)], axis=0).astype(_bf16)
                v = jnp.concatenate([rows(v_ref, prev), rows(v_ref, cur)], axis=0).astype(_bf16)
                places.append((classes, local))
                vs.append(jnp.concatenate([_by_head(v, head_lanes), ones_by_head], axis=1))
                biases.append(bias_ref[branch, 1] if local >= chunk
                              else bias_ref[branch, (cur >= chunk).astype(jnp.int32)])
                for h in range(2):
                    q_head = jnp.where(head_lanes[h], q, jnp.zeros_like(q))
                    logits[u, h] = lax.dot_general(q_head, k, (((1,), (1,)), ((), ())),
                                                   preferred_element_type=_f32)
            return branch, chunk, places, vs, biases, logits

        stages += [functools.partial(score_stage, g) for g in range(r * blocks_per_class // group)]

    def finish_stage(branch, chunk, places, vs, biases, logits):
        probs, maxes = {}, {}
        for u in range(group):
            for h in range(2):
                z = logits[u, h] + biases[u]
                maxes[u, h] = jnp.max(z, axis=1, keepdims=True)
                probs[u, h] = jnp.exp(z - maxes[u, h]).astype(_bf16)
        for u in range(group):
            classes, local = places[u]
            pv_l = jnp.dot(jnp.concatenate([probs[u, 0], probs[u, 1]], axis=1), vs[u],
                           preferred_element_type=_f32)
            pv_t, l_t = pv_l[:, :LANES], pv_l[:, LANES:]
            m_t = jnp.where(head_lanes[0], maxes[u, 0], maxes[u, 1])
            for m, c in enumerate(classes):
                part = slice(m * chunk, (m + 1) * chunk)
                place = (c, pl.ds(local, chunk), slice(None))
                if branch == 0:
                    m_ref[place] = m_t[part]
                    l_ref[place] = l_t[part]
                    acc_ref[place] = pv_t[part]
                else:
                    m_old = m_ref[place]
                    m_new = jnp.maximum(m_old, m_t[part])
                    a_old = jnp.exp(m_old - m_new)
                    a_new = jnp.exp(m_t[part] - m_new)
                    m_ref[place] = m_new
                    l_ref[place] = a_old * l_ref[place] + a_new * l_t[part]
                    acc_ref[place] = a_old * acc_ref[place] + a_new * pv_t[part]

    scores = stages[0]()
    for nxt in stages[1:] + [None]:
        ahead = nxt() if nxt is not None else None
        finish_stage(*scores)
        scores = ahead

    o_ref[...] = (acc_ref[...] / l_ref[...]).astype(o_ref.dtype)


def _dilated(q, kv, *, batch, seq, d_model, group=4):
    pairs = d_model // LANES
    per_class = seq // N_CLASSES
    assert per_class % BLOCK == 0 and N_CLASSES % group == 0
    return pl.pallas_call(
        functools.partial(_dilated_kernel, group=group),
        grid=(batch, pairs, per_class // BLOCK),
        in_specs=[
            pl.BlockSpec((None, N_CLASSES, BLOCK, LANES), lambda b, p, i: (b, 0, i, p)),
            pl.BlockSpec((None, N_CLASSES, per_class, LANES), lambda b, p, i: (b, 0, 0, p)),
            pl.BlockSpec((None, N_CLASSES, per_class, LANES), lambda b, p, i: (b, 0, 0, pairs + p)),
        ],
        out_specs=pl.BlockSpec((None, N_CLASSES, BLOCK, LANES), lambda b, p, i: (b, 0, i, p)),
        out_shape=jax.ShapeDtypeStruct((batch, N_CLASSES, per_class, d_model), _bf16),
        scratch_shapes=[pltpu.VMEM((N_CLASSES, BLOCK, LANES), _f32)] * 3
        + [pltpu.VMEM((len(DILATED_BRANCHES), 2, BLOCK, 2 * BLOCK), _f32)],
        compiler_params=_compiler_params(("parallel", "parallel", "arbitrary")),
        name="dilated",
    )(q, kv, kv)


def _post_attn_kernel(h_ref, o_ref, wo_ref, g_ref, wg_ref, wu_ref, wd_ref, gf_ref, out_ref, acc_ref,
                      *, tf, final_norm):
    h1 = h_ref[...] + jnp.dot(o_ref[...], wo_ref[...], preferred_element_type=_f32)
    xn = _rms(h1, g_ref[...]).astype(_bf16)
    acc_ref[...] = h1
    for c in range(wg_ref.shape[1] // tf):
        cols = slice(c * tf, (c + 1) * tf)
        gate = jnp.dot(xn, wg_ref[:, cols], preferred_element_type=_f32)
        up = jnp.dot(xn, wu_ref[:, cols], preferred_element_type=_f32)
        act = (gate / (1.0 + jnp.exp(-gate)) * up).astype(_bf16)
        acc_ref[...] += jnp.dot(act, wd_ref[cols, :], preferred_element_type=_f32)
    out = acc_ref[...]
    if final_norm:
        out = _rms(out, gf_ref[...])
    out_ref[...] = out


def _post_attn(h, o, wo, g, wg, wu, wd, g_final, *, final_norm, tm=512, tf=256):
    m, d = h.shape
    f = wg.shape[1]
    resident = lambda shape: pl.BlockSpec(shape, lambda i: (0, 0), pipeline_mode=pl.Buffered(1))
    return pl.pallas_call(
        functools.partial(_post_attn_kernel, tf=tf, final_norm=final_norm),
        grid=(m // tm,),
        in_specs=[
            pl.BlockSpec((tm, d), lambda i: (i, 0)),
            pl.BlockSpec((tm, d), lambda i: (i, 0)),
            resident((d, d)),
            resident((1, d)),
            resident((d, f)),
            resident((d, f)),
            resident((f, d)),
            resident((1, d)),
        ],
        out_specs=pl.BlockSpec((tm, d), lambda i: (i, 0)),
        out_shape=jax.ShapeDtypeStruct((m, d), _f32),
        scratch_shapes=[pltpu.VMEM((tm, d), _f32)],
        compiler_params=_compiler_params(("parallel",)),
        name="post_attn",
    )(h, o, wo, g, wg, wu, wd, g_final)


def _rope_tables(seq):
    inv_freq = ROPE_THETA ** (-jnp.arange(0, HEAD_DIM, 2, dtype=_f32) / HEAD_DIM)
    ang = jnp.arange(seq, dtype=_f32)[:, None] * inv_freq[None, :]
    cos, sin = jnp.cos(ang), jnp.sin(ang)
    reps = LANES // HEAD_DIM
    return (jnp.tile(jnp.concatenate([cos, cos], axis=1), (1, reps)),
            jnp.tile(jnp.concatenate([-sin, sin], axis=1), (1, reps)))


def _class_major(a, lead):
    s, f = a.shape[-2:]
    return jnp.swapaxes(a.reshape(*lead, s // N_CLASSES, N_CLASSES, f), -3, -2)


def kernel(x, norm_mix, w_qkv_a, w_o_a, norm_kv, w_kv, w_q_b, w_o_b, norm_ffn, w_gate, w_up, w_down,
           norm_final):
    b, s, d = x.shape
    depth = norm_mix.shape[0]
    n_a = w_qkv_a.shape[0]
    assert d % LANES == 0 and s % (N_CLASSES * BLOCK) == 0
    cos, sin = _rope_tables(s)
    cos_cm, sin_cm = (_class_major(t, ()).reshape(s, LANES) for t in (cos, sin))
    cast = lambda w: w.astype(_bf16)
    w_qkv_a, w_o_a, w_kv, w_q_b, w_o_b = map(cast, (w_qkv_a, w_o_a, w_kv, w_q_b, w_o_b))
    w_gate, w_up, w_down = map(cast, (w_gate, w_up, w_down))
    g_final = norm_final.reshape(1, d)
    per_class = s // N_CLASSES

    h = x.reshape(b * s, d)
    kv = None
    for layer in range(depth):
        g_mix = norm_mix[layer].reshape(1, d)
        if layer < n_a:
            qkv = _norm_proj(h, g_mix, w_qkv_a[layer], cos, sin, rope_cols=0, out_dtype=_bf16, seq=s)
            o = _stickbreak(qkv.reshape(b, s, 3 * d), batch=b, seq=s, d_model=d)
            w_o = w_o_a[layer]
        else:
            j = layer - n_a
            if kv is None:
                h = _class_major(h.reshape(b, s, d), (b,)).reshape(b * s, d)
                kv = _norm_proj(h, norm_kv.reshape(1, d), w_kv, cos_cm, sin_cm, rope_cols=d, out_dtype=_f32,
                                seq=s).reshape(b, N_CLASSES, per_class, 2 * d)
            q = _norm_proj(h, g_mix, w_q_b[j], cos_cm, sin_cm, rope_cols=d, out_dtype=_f32, seq=s)
            o = _dilated(q.reshape(b, N_CLASSES, per_class, d), kv, batch=b, seq=s, d_model=d)
            w_o = w_o_b[j]
        h = _post_attn(h, o.reshape(b * s, d), w_o, norm_ffn[layer].reshape(1, d), w_gate[layer],
                       w_up[layer], w_down[layer], g_final, final_norm=(layer == depth - 1))
    if kv is not None:
        h = jnp.swapaxes(h.reshape(b, N_CLASSES, per_class, d), 1, 2)
    return h.reshape(b, s, d)
```

```python
import functools

import jax
import jax.numpy as jnp
from jax import lax
from jax.experimental import pallas as pl
from jax.experimental.pallas import tpu as pltpu

HEAD_DIM = 64
BLOCK = 128
ROPE_THETA = 10000.0
NORM_EPS = 1e-6
DILATED_BRANCHES = ((128, 1), (512, 4), (2048, 16))
N_CLASSES = max(r for _, r in DILATED_BRANCHES)

LANES = 128
VMEM_LIMIT_BYTES = 56 * 1024 * 1024
MASK_VALUE = -1e30
EXP2_UNDERFLOW = -160.0
LOG2_E = 1.4426950408889634
NEAR_KEYS = 176
MXU_WIDTH = 256
_f32 = jnp.float32
_bf16 = jnp.bfloat16


def _rms(x, g):
    return x * lax.rsqrt(jnp.mean(x * x, axis=-1, keepdims=True) + NORM_EPS) * g


def _compiler_params(semantics):
    return pltpu.CompilerParams(dimension_semantics=semantics, vmem_limit_bytes=VMEM_LIMIT_BYTES)


def _head_lanes():
    lane = lax.broadcasted_iota(jnp.int32, (1, LANES), 1)
    return lane < HEAD_DIM, lane >= HEAD_DIM


def _by_head(x, head_lanes):
    return jnp.concatenate([jnp.where(hm, x, jnp.zeros_like(x)) for hm in head_lanes], axis=0)


def _norm_proj_kernel(x_ref, g_ref, w_ref, cos_ref, sin_ref, o_ref, *, rope_cols):
    xn = _rms(x_ref[...], g_ref[...]).astype(_bf16)
    n_out = o_ref.shape[1]
    tn = MXU_WIDTH
    if rope_cols:
        reps = tn // LANES
        cos = jnp.concatenate([cos_ref[...]] * reps, axis=1)
        sin = jnp.concatenate([sin_ref[...]] * reps, axis=1)
        first_half = (lax.broadcasted_iota(jnp.int32, cos.shape, 1) % HEAD_DIM) < HEAD_DIM // 2
    for c in range(n_out // tn):
        cols = slice(c * tn, (c + 1) * tn)
        y = jnp.dot(xn, w_ref[:, cols], preferred_element_type=_f32)
        if c * tn < rope_cols:
            partner = jnp.where(first_half, pltpu.roll(y, tn - HEAD_DIM // 2, 1),
                                pltpu.roll(y, HEAD_DIM // 2, 1))
            y = y * cos + partner * sin
        o_ref[:, cols] = y.astype(o_ref.dtype)


def _norm_proj(x, g, w, cos, sin, *, rope_cols, out_dtype, seq, tm=512):
    m, d = x.shape
    n_out = w.shape[1]
    pos_blocks = seq // tm
    return pl.pallas_call(
        functools.partial(_norm_proj_kernel, rope_cols=rope_cols),
        grid=(m // tm,),
        in_specs=[
            pl.BlockSpec((tm, d), lambda i: (i, 0)),
            pl.BlockSpec((1, d), lambda i: (0, 0)),
            pl.BlockSpec((d, n_out), lambda i: (0, 0)),
            pl.BlockSpec((tm, LANES), lambda i: (i % pos_blocks, 0)),
            pl.BlockSpec((tm, LANES), lambda i: (i % pos_blocks, 0)),
        ],
        out_specs=pl.BlockSpec((tm, n_out), lambda i: (i, 0)),
        out_shape=jax.ShapeDtypeStruct((m, n_out), out_dtype),
        compiler_params=_compiler_params(("parallel",)),
        name="norm_proj",
    )(x, g, w, cos, sin)


def _stickbreak_kernel(q_ref, k_ref, v_ref, uu_ref, o_ref, acc_ref, carry_ref, c_max_ref,
                       *, tq, tk, per_trip):
    def start_tile(qi, slot=0, first_tile=False, older=False):
        rows = pl.ds(qi * tq if first_tile else pl.multiple_of(qi * tq, tq), tq)
        return _stickbreak_tile(qi, q_ref.at[rows], k_ref, v_ref, uu_ref, o_ref.at[rows], acc_ref, carry_ref,
                                c_max_ref, tq=tq, tk=tk, slot=slot, first_tile=first_tile, older=older)

    def tiles(trip, carry):
        first = 1 + trip * per_trip
        finish = start_tile(first, 0)
        for j in range(per_trip):
            ahead = start_tile(first + j + 1, j + 1) if j + 1 < per_trip else None
            finish()
            finish = ahead

        def older_keys(j, c):
            @pl.when(c_max_ref[j] > EXP2_UNDERFLOW)
            def _():
                start_tile(first + j, j, older=True)
            return c

        lax.fori_loop(0, per_trip, older_keys, 0)
        return carry

    n_tiles = q_ref.shape[0] // tq
    assert (n_tiles - 1) % per_trip == 0
    start_tile(0, first_tile=True)()
    lax.fori_loop(0, (n_tiles - 1) // per_trip, tiles, 0)


def _stickbreak_tile(qi, q_ref, k_ref, v_ref, uu_ref, o_ref, acc_ref, carry_ref, c_max_ref,
                     *, tq, tk, slot, first_tile, older):
    head_lanes = _head_lanes()
    q = q_ref[...] * jnp.asarray(HEAD_DIM ** -0.5, _bf16)
    uu = uu_ref[...]
    strict = (lax.broadcasted_iota(jnp.int32, (tk, tk), 1)
              < lax.broadcasted_iota(jnp.int32, (tk, tk), 0))
    sign_bit = jnp.uint32(0x80000000)

    def mask_top(a, diagonal):
        if not diagonal:
            return a
        top = jnp.where(strict, a[:tk], 0.0)
        return top if a.shape[0] == tk else jnp.concatenate([top, a[tk:]], axis=0)

    def add_rows(x, r0, r1, delta):
        parts = ([x[:r0]] if r0 else []) + [x[r0:r1] + delta] + ([x[r1:]] if r1 < tq else [])
        return parts[0] if len(parts) == 1 else jnp.concatenate(parts, axis=0)

    def score_blocks(blocks):
        values, logits, afters = [], {}, {}
        for b, (jb, r0, r1, _) in enumerate(blocks):
            start = pl.multiple_of(jb * tk, tk)
            z = lax.dot_general(q[r0:r1], _by_head(k_ref[pl.ds(start, tk), :], head_lanes),
                                (((1,), (1,)), ((), ())), preferred_element_type=_f32)
            logits[b, 0], logits[b, 1] = z[:, :tk] * LOG2_E, z[:, tk:] * LOG2_E
            values.append(_by_head(v_ref[pl.ds(start, tk), :], head_lanes))
        for b, (_, _, _, diagonal) in enumerate(blocks):
            for h in range(2):
                z = logits[b, h]
                neg_abs = lax.bitcast_convert_type(lax.bitcast_convert_type(z, jnp.uint32) | sign_bit, _f32)
                softplus = jnp.maximum(z, 0.0) + jnp.log2(1.0 + jnp.exp2(neg_abs))
                sp = mask_top(softplus, diagonal)
                hi = sp.astype(_bf16)
                lo = (sp - hi.astype(_f32)).astype(_bf16)
                afters[b, h] = jnp.dot(jnp.concatenate([hi, lo], axis=1), uu,
                                       preferred_element_type=_f32)
        return values, logits, afters

    def weigh_blocks(blocks, scores, state):
        values, logits, afters = scores
        carries, acc = list(state[:2]), state[2]
        for b, (_, r0, r1, diagonal) in enumerate(blocks):
            weights = []
            for h in range(2):
                after = afters[b, h]
                w = mask_top(jnp.exp2(logits[b, h] + after + carries[h][r0:r1]), diagonal)
                weights.append(w.astype(_bf16))
                carries[h] = add_rows(carries[h], r0, r1, jnp.broadcast_to(after[:, :1], after.shape))
            pv = jnp.dot(jnp.concatenate(weights, axis=1), values[b], preferred_element_type=_f32)
            acc = add_rows(acc, r0, r1, pv)
        return carries[0], carries[1], acc

    def key_blocks(blocks, state):
        return weigh_blocks(blocks, score_blocks(blocks), state)

    blocks_per_tile = tq // tk
    first_diag = qi * blocks_per_tile

    skipped_from = [min(tq, max(NEAR_KEYS - d * tk, 0)) for d in range(blocks_per_tile)]
    partly_skipped = [d for d in range(blocks_per_tile) if skipped_from[d] < tq]

    near_blocks = [(first_diag + d, d * tk, tq, True) for d in reversed(range(blocks_per_tile))]
    if not first_tile:
        near_blocks += [(first_diag - 1 - d, 0, skipped_from[d], False)
                        for d in range(blocks_per_tile) if skipped_from[d]]
    def carry_max(state):
        return jnp.max(jnp.maximum(state[0], state[1]))

    def save(state):
        carry_ref[slot, 0], carry_ref[slot, 1], acc_ref[slot] = state
        return state

    def load():
        return carry_ref[slot, 0], carry_ref[slot, 1], acc_ref[slot]

    def more(loop):
        t, c_max = loop
        return (t < qi) & (c_max > EXP2_UNDERFLOW)

    def older_keys(loop):
        t, _ = loop
        blocks = [(first_diag - 1 - t * blocks_per_tile - d, 0, tq, False) for d in range(blocks_per_tile)]
        return t + 1, carry_max(save(key_blocks(blocks, load())))

    if older:
        for d in partly_skipped:
            save(key_blocks([(first_diag - 1 - d, skipped_from[d], tq, False)], load()))
        lax.while_loop(more, older_keys, (jnp.int32(1), c_max_ref[slot]))
        o_ref[...] = acc_ref[slot].astype(o_ref.dtype)
        return None

    near_scores = score_blocks(near_blocks)

    def finish():
        zero = jnp.zeros((tq, LANES), _f32)
        state = weigh_blocks(near_blocks, near_scores, (zero, zero, zero))
        o_ref[...] = state[2].astype(o_ref.dtype)
        if not first_tile:
            c_max_ref[slot] = carry_max(save(state))

    return finish


def _cumsum_matrix(tk):
    src = jnp.arange(2 * tk)[:, None] % tk
    dst = jnp.arange(tk)[None, :]
    return -(src >= dst).astype(_bf16)


def _stickbreak(qkv, *, batch, seq, d_model, tq=256, tk=128, per_trip=15):
    pairs = d_model // LANES
    return pl.pallas_call(
        functools.partial(_stickbreak_kernel, tq=tq, tk=tk, per_trip=per_trip),
        grid=(batch, pairs),
        in_specs=[
            pl.BlockSpec((None, seq, LANES), lambda b, p: (b, 0, p)),
            pl.BlockSpec((None, seq, LANES), lambda b, p: (b, 0, pairs + p)),
            pl.BlockSpec((None, seq, LANES), lambda b, p: (b, 0, 2 * pairs + p)),
            pl.BlockSpec((2 * tk, tk), lambda b, p: (0, 0)),
        ],
        out_specs=pl.BlockSpec((None, seq, LANES), lambda b, p: (b, 0, p)),
        out_shape=jax.ShapeDtypeStruct((batch, seq, d_model), _bf16),
        scratch_shapes=[pltpu.VMEM((per_trip, tq, LANES), _f32), pltpu.VMEM((per_trip, 2, tq, tk), _f32),
                        pltpu.SMEM((per_trip,), _f32)],
        compiler_params=_compiler_params(("parallel", "parallel")),
        name="stickbreak",
    )(qkv, qkv, qkv, _cumsum_matrix(tk))


def _dilated_kernel(q_ref, k_ref, v_ref, o_ref, m_ref, l_ref, acc_ref, bias_ref, *, group):
    per_class = q_ref.shape[1]
    tile_start = pl.program_id(2) * per_class
    head_lanes = _head_lanes()
    scale = jnp.asarray(HEAD_DIM ** -0.5, _bf16)
    row = lax.broadcasted_iota(jnp.int32, (BLOCK, 2 * BLOCK), 0)
    col = lax.broadcasted_iota(jnp.int32, (BLOCK, 2 * BLOCK), 1)
    in_prev = col < BLOCK
    ones_by_head = ((lax.broadcasted_iota(jnp.int32, (4 * BLOCK, LANES), 0) < 2 * BLOCK)
                    == (lax.broadcasted_iota(jnp.int32, (4 * BLOCK, LANES), 1) < HEAD_DIM)
                    ).astype(_f32).astype(_bf16)
    stages = []

    for branch, (window, r) in enumerate(DILATED_BRANCHES):
        assert window // r == BLOCK
        n_sub = N_CLASSES // r
        chunk = BLOCK // n_sub
        blocks_per_class = per_class // chunk
        sub_index = lambda a: n_sub * (a % chunk) + a // chunk
        q_sub, k_sub = sub_index(row), sub_index(col % BLOCK)
        band_prev = in_prev & (k_sub >= q_sub)
        band_cur = (~in_prev) & (k_sub <= q_sub)
        bias_ref[branch, 0] = jnp.where(band_cur, 0.0, MASK_VALUE)
        bias_ref[branch, 1] = jnp.where(band_prev | band_cur, 0.0, MASK_VALUE)

        def score_stage(g, r=r, branch=branch, n_sub=n_sub, chunk=chunk, blocks_per_class=blocks_per_class):
            places, vs, biases, logits = [], [], [], {}
            for u in range(group):
                idx = g * group + u
                c_r = idx // blocks_per_class
                local = (idx % blocks_per_class) * chunk
                cur = tile_start + local
                prev = jnp.maximum(cur - chunk, 0)
                classes = [c_r + r * m for m in range(n_sub)]

                def rows(ref, start):
                    start = start if isinstance(start, int) else pl.multiple_of(start, chunk)
                    return jnp.concatenate([ref[c, pl.ds(start, chunk), :] for c in classes], axis=0)

                q = rows(q_ref, local).astype(_bf16) * scale
                k = jnp.concatenate([rows(k_ref, prev), rows(k_ref, cur)], axis=0).astype(_bf16)
                v = jnp.concatenate([rows(v_ref, prev), rows(v_ref, cur)], axis=0).astype(_bf16)
                places.append((classes, local))
                vs.append(jnp.concatenate([_by_head(v, head_lanes), ones_by_head], axis=1))
                biases.append(bias_ref[branch, 1] if local >= chunk
                              else bias_ref[branch, (cur >= chunk).astype(jnp.int32)])
                for h in range(2):
                    q_head = jnp.where(head_lanes[h], q, jnp.zeros_like(q))
                    logits[u, h] = lax.dot_general(q_head, k, (((1,), (1,)), ((), ())),
                                                   preferred_element_type=_f32)
            return branch, chunk, places, vs, biases, logits

        stages += [functools.partial(score_stage, g) for g in range(r * blocks_per_class // group)]

    def finish_stage(branch, chunk, places, vs, biases, logits):
        probs, maxes = {}, {}
        for u in range(group):
            for h in range(2):
                z = logits[u, h] + biases[u]
                maxes[u, h] = jnp.max(z, axis=1, keepdims=True)
                probs[u, h] = jnp.exp(z - maxes[u, h]).astype(_bf16)
        for u in range(group):
            classes, local = places[u]
            pv_l = jnp.dot(jnp.concatenate([probs[u, 0], probs[u, 1]], axis=1), vs[u],
                           preferred_element_type=_f32)
            pv_t, l_t = pv_l[:, :LANES], pv_l[:, LANES:]
            m_t = jnp.where(head_lanes[0], maxes[u, 0], maxes[u, 1])
            for m, c in enumerate(classes):
                part = slice(m * chunk, (m + 1) * chunk)
                place = (c, pl.ds(local, chunk), slice(None))
                if branch == 0:
                    m_ref[place] = m_t[part]
                    l_ref[place] = l_t[part]
                    acc_ref[place] = pv_t[part]
                else:
                    m_old = m_ref[place]
                    m_new = jnp.maximum(m_old, m_t[part])
                    a_old = jnp.exp(m_old - m_new)
                    a_new = jnp.exp(m_t[part] - m_new)
                    m_ref[place] = m_new
                    l_ref[place] = a_old * l_ref[place] + a_new * l_t[part]
                    acc_ref[place] = a_old * acc_ref[place] + a_new * pv_t[part]

    scores = stages[0]()
    for nxt in stages[1:] + [None]:
        ahead = nxt() if nxt is not None else None
        finish_stage(*scores)
        scores = ahead

    o_ref[...] = (acc_ref[...] / l_ref[...]).astype(o_ref.dtype)


def _dilated(q, kv, *, batch, seq, d_model, group=4):
    pairs = d_model // LANES
    per_class = seq // N_CLASSES
    assert per_class % BLOCK == 0 and N_CLASSES % group == 0
    return pl.pallas_call(
        functools.partial(_dilated_kernel, group=group),
        grid=(batch, pairs, per_class // BLOCK),
        in_specs=[
            pl.BlockSpec((None, N_CLASSES, BLOCK, LANES), lambda b, p, i: (b, 0, i, p)),
            pl.BlockSpec((None, N_CLASSES, per_class, LANES), lambda b, p, i: (b, 0, 0, p)),
            pl.BlockSpec((None, N_CLASSES, per_class, LANES), lambda b, p, i: (b, 0, 0, pairs + p)),
        ],
        out_specs=pl.BlockSpec((None, N_CLASSES, BLOCK, LANES), lambda b, p, i: (b, 0, i, p)),
        out_shape=jax.ShapeDtypeStruct((batch, N_CLASSES, per_class, d_model), _bf16),
        scratch_shapes=[pltpu.VMEM((N_CLASSES, BLOCK, LANES), _f32)] * 3
        + [pltpu.VMEM((len(DILATED_BRANCHES), 2, BLOCK, 2 * BLOCK), _f32)],
        compiler_params=_compiler_params(("parallel", "parallel", "arbitrary")),
        name="dilated",
    )(q, kv, kv)


def _post_attn_kernel(h_ref, o_ref, wo_ref, g_ref, wg_ref, wu_ref, wd_ref, gf_ref, out_ref, acc_ref,
                      *maybe_rows_ref, tf, final_norm, reorder):
    tm, d = acc_ref.shape
    rows_per_class = tm // N_CLASSES
    h_in, o_in = h_ref[...], o_ref[...]
    if reorder == "from_class_major":
        h_in, o_in = h_in.reshape(tm, d), o_in.reshape(tm, d)
    h1 = h_in + jnp.dot(o_in, wo_ref[...], preferred_element_type=_f32)
    xn = _rms(h1, g_ref[...]).astype(_bf16)
    acc_ref[...] = h1
    for c in range(wg_ref.shape[1] // tf):
        cols = slice(c * tf, (c + 1) * tf)
        gate = jnp.dot(xn, wg_ref[:, cols], preferred_element_type=_f32)
        up = jnp.dot(xn, wu_ref[:, cols], preferred_element_type=_f32)
        act = (gate / (1.0 + jnp.exp(-gate)) * up).astype(_bf16)
        acc_ref[...] += jnp.dot(act, wd_ref[cols, :], preferred_element_type=_f32)
    out = acc_ref[...]
    if final_norm:
        out = _rms(out, gf_ref[...])
    if reorder is None:
        out_ref[...] = out
        return
    rows_ref, = maybe_rows_ref
    for j in range(d // LANES):
        lanes = slice(j * LANES, (j + 1) * LANES)
        for c in range(N_CLASSES):
            strided = pl.ds(c, rows_per_class, stride=N_CLASSES)
            if reorder == "to_class_major":
                if c == 0:
                    rows_ref[j] = out[:, lanes]
                out_ref[c, :, lanes] = rows_ref[j, strided, :]
            else:
                rows_ref[j, strided, :] = out[c * rows_per_class:(c + 1) * rows_per_class, lanes]
        if reorder == "from_class_major":
            out_ref[:, lanes] = rows_ref[j]


def _post_attn(h, o, wo, g, wg, wu, wd, g_final, *, final_norm, seq, reorder=None, tm=512, tf=256):
    m, d = h.shape
    f = wg.shape[1]
    assert seq % tm == 0 and tm % (8 * N_CLASSES) == 0 and f % tf == 0
    tiles = seq // tm
    per_class = seq // N_CLASSES
    resident = lambda shape: pl.BlockSpec(shape, lambda i: (0, 0), pipeline_mode=pl.Buffered(1))
    token_rows = pl.BlockSpec((tm, d), lambda i: (i, 0))
    class_rows = pl.BlockSpec((None, N_CLASSES, tm // N_CLASSES, d), lambda i: (i // tiles, 0, i % tiles, 0))
    class_major = lambda a: a.reshape(m // seq, N_CLASSES, per_class, d)
    rows_in = token_rows
    rows_out, out_shape = token_rows, jax.ShapeDtypeStruct((m, d), _f32)
    if reorder == "to_class_major":
        rows_out, out_shape = class_rows, jax.ShapeDtypeStruct((m // seq, N_CLASSES, per_class, d), _f32)
    elif reorder == "from_class_major":
        rows_in, h, o = class_rows, class_major(h), class_major(o)
    out = pl.pallas_call(
        functools.partial(_post_attn_kernel, tf=tf, final_norm=final_norm, reorder=reorder),
        grid=(m // tm,),
        in_specs=[
            rows_in,
            rows_in,
            resident((d, d)),
            resident((1, d)),
            resident((d, f)),
            resident((d, f)),
            resident((f, d)),
            resident((1, d)),
        ],
        out_specs=rows_out,
        out_shape=out_shape,
        scratch_shapes=[pltpu.VMEM((tm, d), _f32)]
        + ([pltpu.VMEM((d // LANES, tm, LANES), _f32)] if reorder else []),
        compiler_params=_compiler_params(("parallel",)),
        name="post_attn",
    )(h, o, wo, g, wg, wu, wd, g_final)
    return out.reshape(m, d)


def _rope_tables(seq):
    inv_freq = ROPE_THETA ** (-jnp.arange(0, HEAD_DIM, 2, dtype=_f32) / HEAD_DIM)
    ang = jnp.arange(seq, dtype=_f32)[:, None] * inv_freq[None, :]
    cos, sin = jnp.cos(ang), jnp.sin(ang)
    reps = LANES // HEAD_DIM
    return (jnp.tile(jnp.concatenate([cos, cos], axis=1), (1, reps)),
            jnp.tile(jnp.concatenate([-sin, sin], axis=1), (1, reps)))


def _class_major(a, lead):
    s, f = a.shape[-2:]
    return jnp.swapaxes(a.reshape(*lead, s // N_CLASSES, N_CLASSES, f), -3, -2)


def kernel(x, norm_mix, w_qkv_a, w_o_a, norm_kv, w_kv, w_q_b, w_o_b, norm_ffn, w_gate, w_up, w_down,
           norm_final):
    b, s, d = x.shape
    depth = norm_mix.shape[0]
    n_a = w_qkv_a.shape[0]
    assert d % LANES == 0 and s % (N_CLASSES * BLOCK) == 0
    cos, sin = _rope_tables(s)
    cos_cm, sin_cm = (_class_major(t, ()).reshape(s, LANES) for t in (cos, sin))
    cast = lambda w: w.astype(_bf16)
    w_qkv_a, w_o_a, w_kv, w_q_b, w_o_b = map(cast, (w_qkv_a, w_o_a, w_kv, w_q_b, w_o_b))
    w_gate, w_up, w_down = map(cast, (w_gate, w_up, w_down))
    g_final = norm_final.reshape(1, d)
    per_class = s // N_CLASSES

    reorder = {}
    if 0 < n_a < depth - 1:
        reorder = {n_a - 1: "to_class_major", depth - 1: "from_class_major"}

    h = x.reshape(b * s, d)
    kv = None
    for layer in range(depth):
        g_mix = norm_mix[layer].reshape(1, d)
        if layer < n_a:
            qkv = _norm_proj(h, g_mix, w_qkv_a[layer], cos, sin, rope_cols=0, out_dtype=_bf16, seq=s)
            o = _stickbreak(qkv.reshape(b, s, 3 * d), batch=b, seq=s, d_model=d)
            w_o = w_o_a[layer]
        else:
            j = layer - n_a
            if kv is None:
                if not reorder:
                    h = _class_major(h.reshape(b, s, d), (b,)).reshape(b * s, d)
                kv = _norm_proj(h, norm_kv.reshape(1, d), w_kv, cos_cm, sin_cm, rope_cols=d, out_dtype=_f32,
                                seq=s).reshape(b, N_CLASSES, per_class, 2 * d)
            q = _norm_proj(h, g_mix, w_q_b[j], cos_cm, sin_cm, rope_cols=d, out_dtype=_f32, seq=s)
            o = _dilated(q.reshape(b, N_CLASSES, per_class, d), kv, batch=b, seq=s, d_model=d)
            w_o = w_o_b[j]
        h = _post_attn(h, o.reshape(b * s, d), w_o, norm_ffn[layer].reshape(1, d), w_gate[layer],
                       w_up[layer], w_down[layer], g_final, final_norm=(layer == depth - 1), seq=s,
                       reorder=reorder.get(layer))
    if kv is not None and not reorder:
        h = jnp.swapaxes(h.reshape(b, N_CLASSES, per_class, d), 1, 2)
    return h.reshape(b, s, d)
```

```python
import functools

import jax
import jax.numpy as jnp
from jax import lax
from jax.experimental import pallas as pl
from jax.experimental.pallas import tpu as pltpu

HEAD_DIM = 64
BLOCK = 128
ROPE_THETA = 10000.0
NORM_EPS = 1e-6
DILATED_BRANCHES = ((128, 1), (512, 4), (2048, 16))
N_CLASSES = max(r for _, r in DILATED_BRANCHES)

LANES = 128
VMEM_LIMIT_BYTES = 56 * 1024 * 1024
MASK_VALUE = -1e30
EXP2_UNDERFLOW = -160.0
LOG2_E = 1.4426950408889634
NEAR_KEYS = 176
MXU_WIDTH = 256
_f32 = jnp.float32
_bf16 = jnp.bfloat16


def _rms(x, g):
    return x * lax.rsqrt(jnp.mean(x * x, axis=-1, keepdims=True) + NORM_EPS) * g


def _compiler_params(semantics):
    return pltpu.CompilerParams(dimension_semantics=semantics, vmem_limit_bytes=VMEM_LIMIT_BYTES)


def _head_lanes():
    lane = lax.broadcasted_iota(jnp.int32, (1, LANES), 1)
    return lane < HEAD_DIM, lane >= HEAD_DIM


def _by_head(x, head_lanes):
    return jnp.concatenate([jnp.where(hm, x, jnp.zeros_like(x)) for hm in head_lanes], axis=0)


def _norm_proj_kernel(x_ref, cos_ref, sin_ref, *refs, rope_cols):
    n_proj = len(rope_cols)
    x = x_ref[...]
    x_hat = x * lax.rsqrt(jnp.mean(x * x, axis=-1, keepdims=True) + NORM_EPS)
    tn = MXU_WIDTH
    if any(rope_cols):
        reps = tn // LANES
        cos = jnp.concatenate([cos_ref[...]] * reps, axis=1)
        sin = jnp.concatenate([sin_ref[...]] * reps, axis=1)
        first_half = (lax.broadcasted_iota(jnp.int32, cos.shape, 1) % HEAD_DIM) < HEAD_DIM // 2
    for p in range(n_proj):
        g_ref, w_ref, o_ref = refs[2 * p], refs[2 * p + 1], refs[2 * n_proj + p]
        xn = (x_hat * g_ref[...]).astype(_bf16)
        for c in range(o_ref.shape[1] // tn):
            cols = slice(c * tn, (c + 1) * tn)
            y = jnp.dot(xn, w_ref[:, cols], preferred_element_type=_f32)
            if c * tn < rope_cols[p]:
                partner = jnp.where(first_half, pltpu.roll(y, tn - HEAD_DIM // 2, 1),
                                    pltpu.roll(y, HEAD_DIM // 2, 1))
                y = y * cos + partner * sin
            o_ref[:, cols] = y.astype(o_ref.dtype)


def _norm_proj(x, projections, cos, sin, *, seq, tm=512):
    m, d = x.shape
    assert m % tm == 0 and seq % tm == 0
    pos_blocks = seq // tm
    operands, in_specs, out_specs, out_shapes = [], [], [], []
    for g, w, rope_cols, out_dtype in projections:
        n_out = w.shape[1]
        assert n_out % MXU_WIDTH == 0 and rope_cols % MXU_WIDTH == 0
        operands += [g, w]
        in_specs += [pl.BlockSpec((1, d), lambda i: (0, 0)), pl.BlockSpec((d, n_out), lambda i: (0, 0))]
        out_specs.append(pl.BlockSpec((tm, n_out), lambda i: (i, 0)))
        out_shapes.append(jax.ShapeDtypeStruct((m, n_out), out_dtype))
    return pl.pallas_call(
        functools.partial(_norm_proj_kernel, rope_cols=tuple(p[2] for p in projections)),
        grid=(m // tm,),
        in_specs=[
            pl.BlockSpec((tm, d), lambda i: (i, 0)),
            pl.BlockSpec((tm, LANES), lambda i: (i % pos_blocks, 0)),
            pl.BlockSpec((tm, LANES), lambda i: (i % pos_blocks, 0)),
        ] + in_specs,
        out_specs=out_specs,
        out_shape=out_shapes,
        compiler_params=_compiler_params(("parallel",)),
        name="norm_proj",
    )(x, cos, sin, *operands)


def _stickbreak_kernel(q_ref, k_ref, v_ref, uu_ref, o_ref, acc_ref, carry_ref, c_max_ref,
                       *, tq, tk, per_trip):
    def start_tile(qi, slot=0, first_tile=False, older=False):
        rows = pl.ds(qi * tq if first_tile else pl.multiple_of(qi * tq, tq), tq)
        return _stickbreak_tile(qi, q_ref.at[rows], k_ref, v_ref, uu_ref, o_ref.at[rows], acc_ref, carry_ref,
                                c_max_ref, tq=tq, tk=tk, slot=slot, first_tile=first_tile, older=older)

    def tiles(trip, carry):
        first = 1 + trip * per_trip
        finish = start_tile(first, 0)
        for j in range(per_trip):
            ahead = start_tile(first + j + 1, j + 1) if j + 1 < per_trip else None
            finish()
            finish = ahead

        def older_keys(j, c):
            @pl.when(c_max_ref[j] > EXP2_UNDERFLOW)
            def _():
                start_tile(first + j, j, older=True)
            return c

        lax.fori_loop(0, per_trip, older_keys, 0)
        return carry

    n_tiles = q_ref.shape[0] // tq
    assert (n_tiles - 1) % per_trip == 0
    start_tile(0, first_tile=True)()
    lax.fori_loop(0, (n_tiles - 1) // per_trip, tiles, 0)


def _stickbreak_tile(qi, q_ref, k_ref, v_ref, uu_ref, o_ref, acc_ref, carry_ref, c_max_ref,
                     *, tq, tk, slot, first_tile, older):
    head_lanes = _head_lanes()
    q = q_ref[...] * jnp.asarray(HEAD_DIM ** -0.5, _bf16)
    uu = uu_ref[...]
    strict = (lax.broadcasted_iota(jnp.int32, (tk, tk), 1)
              < lax.broadcasted_iota(jnp.int32, (tk, tk), 0))
    sign_bit = jnp.uint32(0x80000000)

    def mask_top(a, diagonal):
        if not diagonal:
            return a
        top = jnp.where(strict, a[:tk], 0.0)
        return top if a.shape[0] == tk else jnp.concatenate([top, a[tk:]], axis=0)

    def add_rows(x, r0, r1, delta):
        parts = ([x[:r0]] if r0 else []) + [x[r0:r1] + delta] + ([x[r1:]] if r1 < tq else [])
        return parts[0] if len(parts) == 1 else jnp.concatenate(parts, axis=0)

    def score_blocks(blocks):
        values, logits, afters = [], {}, {}
        for b, (jb, r0, r1, _) in enumerate(blocks):
            start = pl.multiple_of(jb * tk, tk)
            z = lax.dot_general(q[r0:r1], _by_head(k_ref[pl.ds(start, tk), :], head_lanes),
                                (((1,), (1,)), ((), ())), preferred_element_type=_f32)
            logits[b, 0], logits[b, 1] = z[:, :tk] * LOG2_E, z[:, tk:] * LOG2_E
            values.append(_by_head(v_ref[pl.ds(start, tk), :], head_lanes))
        for b, (_, _, _, diagonal) in enumerate(blocks):
            for h in range(2):
                z = logits[b, h]
                neg_abs = lax.bitcast_convert_type(lax.bitcast_convert_type(z, jnp.uint32) | sign_bit, _f32)
                softplus = jnp.maximum(z, 0.0) + jnp.log2(1.0 + jnp.exp2(neg_abs))
                sp = mask_top(softplus, diagonal)
                hi = sp.astype(_bf16)
                lo = (sp - hi.astype(_f32)).astype(_bf16)
                afters[b, h] = jnp.dot(jnp.concatenate([hi, lo], axis=1), uu,
                                       preferred_element_type=_f32)
        return values, logits, afters

    def weigh_blocks(blocks, scores, state):
        values, logits, afters = scores
        carries, acc = list(state[:2]), state[2]
        for b, (_, r0, r1, diagonal) in enumerate(blocks):
            weights = []
            for h in range(2):
                after = afters[b, h]
                w = mask_top(jnp.exp2(logits[b, h] + after + carries[h][r0:r1]), diagonal)
                weights.append(w.astype(_bf16))
                carries[h] = add_rows(carries[h], r0, r1, jnp.broadcast_to(after[:, :1], after.shape))
            pv = jnp.dot(jnp.concatenate(weights, axis=1), values[b], preferred_element_type=_f32)
            acc = add_rows(acc, r0, r1, pv)
        return carries[0], carries[1], acc

    def key_blocks(blocks, state):
        return weigh_blocks(blocks, score_blocks(blocks), state)

    blocks_per_tile = tq // tk
    first_diag = qi * blocks_per_tile

    skipped_from = [min(tq, max(NEAR_KEYS - d * tk, 0)) for d in range(blocks_per_tile)]
    partly_skipped = [d for d in range(blocks_per_tile) if skipped_from[d] < tq]

    near_blocks = [(first_diag + d, d * tk, tq, True) for d in reversed(range(blocks_per_tile))]
    if not first_tile:
        near_blocks += [(first_diag - 1 - d, 0, skipped_from[d], False)
                        for d in range(blocks_per_tile) if skipped_from[d]]
    def carry_max(state):
        return jnp.max(jnp.maximum(state[0], state[1]))

    def save(state):
        carry_ref[slot, 0], carry_ref[slot, 1], acc_ref[slot] = state
        return state

    def load():
        return carry_ref[slot, 0], carry_ref[slot, 1], acc_ref[slot]

    def more(loop):
        t, c_max = loop
        return (t < qi) & (c_max > EXP2_UNDERFLOW)

    def older_keys(loop):
        t, _ = loop
        blocks = [(first_diag - 1 - t * blocks_per_tile - d, 0, tq, False) for d in range(blocks_per_tile)]
        return t + 1, carry_max(save(key_blocks(blocks, load())))

    if older:
        for d in partly_skipped:
            save(key_blocks([(first_diag - 1 - d, skipped_from[d], tq, False)], load()))
        lax.while_loop(more, older_keys, (jnp.int32(1), c_max_ref[slot]))
        o_ref[...] = acc_ref[slot].astype(o_ref.dtype)
        return None

    near_scores = score_blocks(near_blocks)

    def finish():
        zero = jnp.zeros((tq, LANES), _f32)
        state = weigh_blocks(near_blocks, near_scores, (zero, zero, zero))
        o_ref[...] = state[2].astype(o_ref.dtype)
        if not first_tile:
            c_max_ref[slot] = carry_max(save(state))

    return finish


def _cumsum_matrix(tk):
    src = jnp.arange(2 * tk)[:, None] % tk
    dst = jnp.arange(tk)[None, :]
    return -(src >= dst).astype(_bf16)


def _stickbreak(qkv, *, batch, seq, d_model, tq=256, tk=128, per_trip=15):
    pairs = d_model // LANES
    return pl.pallas_call(
        functools.partial(_stickbreak_kernel, tq=tq, tk=tk, per_trip=per_trip),
        grid=(batch, pairs),
        in_specs=[
            pl.BlockSpec((None, seq, LANES), lambda b, p: (b, 0, p)),
            pl.BlockSpec((None, seq, LANES), lambda b, p: (b, 0, pairs + p)),
            pl.BlockSpec((None, seq, LANES), lambda b, p: (b, 0, 2 * pairs + p)),
            pl.BlockSpec((2 * tk, tk), lambda b, p: (0, 0)),
        ],
        out_specs=pl.BlockSpec((None, seq, LANES), lambda b, p: (b, 0, p)),
        out_shape=jax.ShapeDtypeStruct((batch, seq, d_model), _bf16),
        scratch_shapes=[pltpu.VMEM((per_trip, tq, LANES), _f32), pltpu.VMEM((per_trip, 2, tq, tk), _f32),
                        pltpu.SMEM((per_trip,), _f32)],
        compiler_params=_compiler_params(("parallel", "parallel")),
        name="stickbreak",
    )(qkv, qkv, qkv, _cumsum_matrix(tk))


def _dilated_kernel(q_ref, k_ref, v_ref, o_ref, m_ref, l_ref, acc_ref, bias_ref, *, group):
    per_class = q_ref.shape[1]
    tile_start = pl.program_id(2) * per_class
    head_lanes = _head_lanes()
    scale = jnp.asarray(HEAD_DIM ** -0.5, _bf16)
    row = lax.broadcasted_iota(jnp.int32, (BLOCK, 2 * BLOCK), 0)
    col = lax.broadcasted_iota(jnp.int32, (BLOCK, 2 * BLOCK), 1)
    in_prev = col < BLOCK
    ones_by_head = ((lax.broadcasted_iota(jnp.int32, (4 * BLOCK, LANES), 0) < 2 * BLOCK)
                    == (lax.broadcasted_iota(jnp.int32, (4 * BLOCK, LANES), 1) < HEAD_DIM)
                    ).astype(_f32).astype(_bf16)
    stages = []

    for branch, (window, r) in enumerate(DILATED_BRANCHES):
        assert window // r == BLOCK
        n_sub = N_CLASSES // r
        chunk = BLOCK // n_sub
        blocks_per_class = per_class // chunk
        sub_index = lambda a: n_sub * (a % chunk) + a // chunk
        q_sub, k_sub = sub_index(row), sub_index(col % BLOCK)
        band_prev = in_prev & (k_sub >= q_sub)
        band_cur = (~in_prev) & (k_sub <= q_sub)
        bias_ref[branch, 0] = jnp.where(band_cur, 0.0, MASK_VALUE)
        bias_ref[branch, 1] = jnp.where(band_prev | band_cur, 0.0, MASK_VALUE)

        def score_stage(g, r=r, branch=branch, n_sub=n_sub, chunk=chunk, blocks_per_class=blocks_per_class):
            places, vs, biases, logits = [], [], [], {}
            for u in range(group):
                idx = g * group + u
                c_r = idx // blocks_per_class
                local = (idx % blocks_per_class) * chunk
                cur = tile_start + local
                prev = jnp.maximum(cur - chunk, 0)
                classes = [c_r + r * m for m in range(n_sub)]

                def rows(ref, start):
                    start = start if isinstance(start, int) else pl.multiple_of(start, chunk)
                    return jnp.concatenate([ref[c, pl.ds(start, chunk), :] for c in classes], axis=0)

                q = rows(q_ref, local).astype(_bf16) * scale
                k = jnp.concatenate([rows(k_ref, prev), rows(k_ref, cur)], axis=0).astype(_bf16)
                v = jnp.concatenate([rows(v_ref, prev), rows(v_ref, cur)], axis=0).astype(_bf16)
                places.append((classes, local))
                vs.append(jnp.concatenate([_by_head(v, head_lanes), ones_by_head], axis=1))
                biases.append(bias_ref[branch, 1] if local >= chunk
                              else bias_ref[branch, (cur >= chunk).astype(jnp.int32)])
                for h in range(2):
                    q_head = jnp.where(head_lanes[h], q, jnp.zeros_like(q))
                    logits[u, h] = lax.dot_general(q_head, k, (((1,), (1,)), ((), ())),
                                                   preferred_element_type=_f32)
            return branch, chunk, places, vs, biases, logits

        stages += [functools.partial(score_stage, g) for g in range(r * blocks_per_class // group)]

    def finish_stage(branch, chunk, places, vs, biases, logits):
        probs, maxes = {}, {}
        for u in range(group):
            for h in range(2):
                z = logits[u, h] + biases[u]
                maxes[u, h] = jnp.max(z, axis=1, keepdims=True)
                probs[u, h] = jnp.exp(z - maxes[u, h]).astype(_bf16)
        for u in range(group):
            classes, local = places[u]
            pv_l = jnp.dot(jnp.concatenate([probs[u, 0], probs[u, 1]], axis=1), vs[u],
                           preferred_element_type=_f32)
            pv_t, l_t = pv_l[:, :LANES], pv_l[:, LANES:]
            m_t = jnp.where(head_lanes[0], maxes[u, 0], maxes[u, 1])
            for m, c in enumerate(classes):
                part = slice(m * chunk, (m + 1) * chunk)
                place = (c, pl.ds(local, chunk), slice(None))
                if branch == 0:
                    m_ref[place] = m_t[part]
                    l_ref[place] = l_t[part]
                    acc_ref[place] = pv_t[part]
                else:
                    m_old = m_ref[place]
                    m_new = jnp.maximum(m_old, m_t[part])
                    a_old = jnp.exp(m_old - m_new)
                    a_new = jnp.exp(m_t[part] - m_new)
                    m_ref[place] = m_new
                    l_ref[place] = a_old * l_ref[place] + a_new * l_t[part]
                    acc_ref[place] = a_old * acc_ref[place] + a_new * pv_t[part]

    scores = stages[0]()
    for nxt in stages[1:] + [None]:
        ahead = nxt() if nxt is not None else None
        finish_stage(*scores)
        scores = ahead

    o_ref[...] = (acc_ref[...] / l_ref[...]).astype(o_ref.dtype)


def _dilated(q, kv, *, batch, seq, d_model, group=4):
    pairs = d_model // LANES
    per_class = seq // N_CLASSES
    assert per_class % BLOCK == 0 and N_CLASSES % group == 0
    return pl.pallas_call(
        functools.partial(_dilated_kernel, group=group),
        grid=(batch, pairs, per_class // BLOCK),
        in_specs=[
            pl.BlockSpec((None, N_CLASSES, BLOCK, LANES), lambda b, p, i: (b, 0, i, p)),
            pl.BlockSpec((None, N_CLASSES, per_class, LANES), lambda b, p, i: (b, 0, 0, p)),
            pl.BlockSpec((None, N_CLASSES, per_class, LANES), lambda b, p, i: (b, 0, 0, pairs + p)),
        ],
        out_specs=pl.BlockSpec((None, N_CLASSES, BLOCK, LANES), lambda b, p, i: (b, 0, i, p)),
        out_shape=jax.ShapeDtypeStruct((batch, N_CLASSES, per_class, d_model), _bf16),
        scratch_shapes=[pltpu.VMEM((N_CLASSES, BLOCK, LANES), _f32)] * 3
        + [pltpu.VMEM((len(DILATED_BRANCHES), 2, BLOCK, 2 * BLOCK), _f32)],
        compiler_params=_compiler_params(("parallel", "parallel", "arbitrary")),
        name="dilated",
    )(q, kv, kv)


def _post_attn_kernel(h_ref, o_ref, wo_ref, g_ref, wg_ref, wu_ref, wd_ref, gf_ref, out_ref, acc_ref,
                      *maybe_rows_ref, tf, final_norm, reorder):
    tm, d = acc_ref.shape
    rows_per_class = tm // N_CLASSES
    h_in, o_in = h_ref[...], o_ref[...]
    if reorder == "from_class_major":
        h_in, o_in = h_in.reshape(tm, d), o_in.reshape(tm, d)
    h1 = h_in + jnp.dot(o_in, wo_ref[...], preferred_element_type=_f32)
    xn = _rms(h1, g_ref[...]).astype(_bf16)
    acc_ref[...] = h1
    for c in range(wg_ref.shape[1] // tf):
        cols = slice(c * tf, (c + 1) * tf)
        gate = jnp.dot(xn, wg_ref[:, cols], preferred_element_type=_f32)
        up = jnp.dot(xn, wu_ref[:, cols], preferred_element_type=_f32)
        act = (gate / (1.0 + jnp.exp(-gate)) * up).astype(_bf16)
        acc_ref[...] += jnp.dot(act, wd_ref[cols, :], preferred_element_type=_f32)
    out = acc_ref[...]
    if final_norm:
        out = _rms(out, gf_ref[...])
    if reorder is None:
        out_ref[...] = out
        return
    rows_ref, = maybe_rows_ref
    for j in range(d // LANES):
        lanes = slice(j * LANES, (j + 1) * LANES)
        for c in range(N_CLASSES):
            strided = pl.ds(c, rows_per_class, stride=N_CLASSES)
            if reorder == "to_class_major":
                if c == 0:
                    rows_ref[j] = out[:, lanes]
                out_ref[c, :, lanes] = rows_ref[j, strided, :]
            else:
                rows_ref[j, strided, :] = out[c * rows_per_class:(c + 1) * rows_per_class, lanes]
        if reorder == "from_class_major":
            out_ref[:, lanes] = rows_ref[j]


def _post_attn(h, o, wo, g, wg, wu, wd, g_final, *, final_norm, seq, reorder=None, tm=512, tf=256):
    m, d = h.shape
    f = wg.shape[1]
    assert seq % tm == 0 and tm % (8 * N_CLASSES) == 0 and f % tf == 0
    tiles = seq // tm
    per_class = seq // N_CLASSES
    resident = lambda shape: pl.BlockSpec(shape, lambda i: (0, 0), pipeline_mode=pl.Buffered(1))
    token_rows = pl.BlockSpec((tm, d), lambda i: (i, 0))
    class_rows = pl.BlockSpec((None, N_CLASSES, tm // N_CLASSES, d), lambda i: (i // tiles, 0, i % tiles, 0))
    class_major = lambda a: a.reshape(m // seq, N_CLASSES, per_class, d)
    rows_in = token_rows
    rows_out, out_shape = token_rows, jax.ShapeDtypeStruct((m, d), _f32)
    if reorder == "to_class_major":
        rows_out, out_shape = class_rows, jax.ShapeDtypeStruct((m // seq, N_CLASSES, per_class, d), _f32)
    elif reorder == "from_class_major":
        rows_in, h, o = class_rows, class_major(h), class_major(o)
    out = pl.pallas_call(
        functools.partial(_post_attn_kernel, tf=tf, final_norm=final_norm, reorder=reorder),
        grid=(m // tm,),
        in_specs=[
            rows_in,
            rows_in,
            resident((d, d)),
            resident((1, d)),
            resident((d, f)),
            resident((d, f)),
            resident((f, d)),
            resident((1, d)),
        ],
        out_specs=rows_out,
        out_shape=out_shape,
        scratch_shapes=[pltpu.VMEM((tm, d), _f32)]
        + ([pltpu.VMEM((d // LANES, tm, LANES), _f32)] if reorder else []),
        compiler_params=_compiler_params(("parallel",)),
        name="post_attn",
    )(h, o, wo, g, wg, wu, wd, g_final)
    return out.reshape(m, d)


def _rope_tables(seq):
    inv_freq = ROPE_THETA ** (-jnp.arange(0, HEAD_DIM, 2, dtype=_f32) / HEAD_DIM)
    ang = jnp.arange(seq, dtype=_f32)[:, None] * inv_freq[None, :]
    cos, sin = jnp.cos(ang), jnp.sin(ang)
    reps = LANES // HEAD_DIM
    return (jnp.tile(jnp.concatenate([cos, cos], axis=1), (1, reps)),
            jnp.tile(jnp.concatenate([-sin, sin], axis=1), (1, reps)))


def _class_major(a, lead):
    s, f = a.shape[-2:]
    return jnp.swapaxes(a.reshape(*lead, s // N_CLASSES, N_CLASSES, f), -3, -2)


def kernel(x, norm_mix, w_qkv_a, w_o_a, norm_kv, w_kv, w_q_b, w_o_b, norm_ffn, w_gate, w_up, w_down,
           norm_final):
    b, s, d = x.shape
    depth = norm_mix.shape[0]
    n_a = w_qkv_a.shape[0]
    assert d % LANES == 0 and s % (N_CLASSES * BLOCK) == 0
    cos, sin = _rope_tables(s)
    cos_cm, sin_cm = (_class_major(t, ()).reshape(s, LANES) for t in (cos, sin))
    cast = lambda w: w.astype(_bf16)
    w_qkv_a, w_o_a, w_kv, w_q_b, w_o_b = map(cast, (w_qkv_a, w_o_a, w_kv, w_q_b, w_o_b))
    w_gate, w_up, w_down = map(cast, (w_gate, w_up, w_down))
    g_final = norm_final.reshape(1, d)
    per_class = s // N_CLASSES

    reorder = {}
    if 0 < n_a < depth - 1:
        reorder = {n_a - 1: "to_class_major", depth - 1: "from_class_major"}

    h = x.reshape(b * s, d)
    kv = None
    for layer in range(depth):
        g_mix = norm_mix[layer].reshape(1, d)
        if layer < n_a:
            qkv, = _norm_proj(h, [(g_mix, w_qkv_a[layer], 0, _bf16)], cos, sin, seq=s)
            o = _stickbreak(qkv.reshape(b, s, 3 * d), batch=b, seq=s, d_model=d)
            w_o = w_o_a[layer]
        else:
            j = layer - n_a
            projections = [(g_mix, w_q_b[j], d, _f32)]
            if kv is None:
                if not reorder:
                    h = _class_major(h.reshape(b, s, d), (b,)).reshape(b * s, d)
                projections.append((norm_kv.reshape(1, d), w_kv, d, _f32))
            q, *new_kv = _norm_proj(h, projections, cos_cm, sin_cm, seq=s)
            if new_kv:
                kv = new_kv[0].reshape(b, N_CLASSES, per_class, 2 * d)
            o = _dilated(q.reshape(b, N_CLASSES, per_class, d), kv, batch=b, seq=s, d_model=d)
            w_o = w_o_b[j]
        h = _post_attn(h, o.reshape(b * s, d), w_o, norm_ffn[layer].reshape(1, d), w_gate[layer],
                       w_up[layer], w_down[layer], g_final, final_norm=(layer == depth - 1), seq=s,
                       reorder=reorder.get(layer))
    if kv is not None and not reorder:
        h = jnp.swapaxes(h.reshape(b, N_CLASSES, per_class, d), 1, 2)
    return h.reshape(b, s, d)
```

```python
import functools

import jax
import jax.numpy as jnp
from jax import lax
from jax.experimental import pallas as pl
from jax.experimental.pallas import tpu as pltpu

HEAD_DIM = 64
BLOCK = 128
ROPE_THETA = 10000.0
NORM_EPS = 1e-6
DILATED_BRANCHES = ((128, 1), (512, 4), (2048, 16))
N_CLASSES = max(r for _, r in DILATED_BRANCHES)

LANES = 128
VMEM_LIMIT_BYTES = 56 * 1024 * 1024
MASK_VALUE = -1e30
EXP2_UNDERFLOW = -160.0
LOG2_E = 1.4426950408889634
NEAR_KEYS = 176
MXU_WIDTH = 256
_f32 = jnp.float32
_bf16 = jnp.bfloat16


def _rms(x, g):
    return x * lax.rsqrt(jnp.mean(x * x, axis=-1, keepdims=True) + NORM_EPS) * g


def _compiler_params(semantics):
    return pltpu.CompilerParams(dimension_semantics=semantics, vmem_limit_bytes=VMEM_LIMIT_BYTES)


def _head_lanes():
    lane = lax.broadcasted_iota(jnp.int32, (1, LANES), 1)
    return lane < HEAD_DIM, lane >= HEAD_DIM


def _by_head(x, head_lanes):
    return jnp.concatenate([jnp.where(hm, x, jnp.zeros_like(x)) for hm in head_lanes], axis=0)


def _norm_project(x, cos_ref, sin_ref, projections):
    x_hat = x * lax.rsqrt(jnp.mean(x * x, axis=-1, keepdims=True) + NORM_EPS)
    tn = MXU_WIDTH
    if any(p[3] for p in projections):
        reps = tn // LANES
        cos = jnp.concatenate([cos_ref[...]] * reps, axis=1)
        sin = jnp.concatenate([sin_ref[...]] * reps, axis=1)
        first_half = (lax.broadcasted_iota(jnp.int32, cos.shape, 1) % HEAD_DIM) < HEAD_DIM // 2
    for g_ref, w_ref, o_ref, rope_cols in projections:
        xn = (x_hat * g_ref[...]).astype(_bf16)
        for c in range(o_ref.shape[1] // tn):
            cols = slice(c * tn, (c + 1) * tn)
            y = jnp.dot(xn, w_ref[:, cols], preferred_element_type=_f32)
            if c * tn < rope_cols:
                partner = jnp.where(first_half, pltpu.roll(y, tn - HEAD_DIM // 2, 1),
                                    pltpu.roll(y, HEAD_DIM // 2, 1))
                y = y * cos + partner * sin
            o_ref[:, cols] = y.astype(o_ref.dtype)


def _norm_proj_kernel(x_ref, cos_ref, sin_ref, *refs, rope_cols):
    n_proj = len(rope_cols)
    _norm_project(x_ref[...], cos_ref, sin_ref,
                  [(refs[2 * p], refs[2 * p + 1], refs[2 * n_proj + p], rope_cols[p]) for p in range(n_proj)])


def _norm_proj(x, projections, cos, sin, *, seq, tm=512):
    m, d = x.shape
    assert m % tm == 0 and seq % tm == 0
    pos_blocks = seq // tm
    operands, in_specs, out_specs, out_shapes = [], [], [], []
    for g, w, rope_cols, out_dtype in projections:
        n_out = w.shape[1]
        assert n_out % MXU_WIDTH == 0 and rope_cols % MXU_WIDTH == 0
        operands += [g, w]
        in_specs += [pl.BlockSpec((1, d), lambda i: (0, 0)), pl.BlockSpec((d, n_out), lambda i: (0, 0))]
        out_specs.append(pl.BlockSpec((tm, n_out), lambda i: (i, 0)))
        out_shapes.append(jax.ShapeDtypeStruct((m, n_out), out_dtype))
    return pl.pallas_call(
        functools.partial(_norm_proj_kernel, rope_cols=tuple(p[2] for p in projections)),
        grid=(m // tm,),
        in_specs=[
            pl.BlockSpec((tm, d), lambda i: (i, 0)),
            pl.BlockSpec((tm, LANES), lambda i: (i % pos_blocks, 0)),
            pl.BlockSpec((tm, LANES), lambda i: (i % pos_blocks, 0)),
        ] + in_specs,
        out_specs=out_specs,
        out_shape=out_shapes,
        compiler_params=_compiler_params(("parallel",)),
        name="norm_proj",
    )(x, cos, sin, *operands)


def _stickbreak_kernel(q_ref, k_ref, v_ref, uu_ref, o_ref, acc_ref, carry_ref, c_max_ref,
                       *, tq, tk, per_trip):
    def start_tile(qi, slot=0, first_tile=False, older=False):
        rows = pl.ds(qi * tq if first_tile else pl.multiple_of(qi * tq, tq), tq)
        return _stickbreak_tile(qi, q_ref.at[rows], k_ref, v_ref, uu_ref, o_ref.at[rows], acc_ref, carry_ref,
                                c_max_ref, tq=tq, tk=tk, slot=slot, first_tile=first_tile, older=older)

    def tiles(trip, carry):
        first = 1 + trip * per_trip
        finish = start_tile(first, 0)
        for j in range(per_trip):
            ahead = start_tile(first + j + 1, j + 1) if j + 1 < per_trip else None
            finish()
            finish = ahead

        def older_keys(j, c):
            @pl.when(c_max_ref[j] > EXP2_UNDERFLOW)
            def _():
                start_tile(first + j, j, older=True)
            return c

        lax.fori_loop(0, per_trip, older_keys, 0)
        return carry

    n_tiles = q_ref.shape[0] // tq
    assert (n_tiles - 1) % per_trip == 0
    start_tile(0, first_tile=True)()
    lax.fori_loop(0, (n_tiles - 1) // per_trip, tiles, 0)


def _stickbreak_tile(qi, q_ref, k_ref, v_ref, uu_ref, o_ref, acc_ref, carry_ref, c_max_ref,
                     *, tq, tk, slot, first_tile, older):
    head_lanes = _head_lanes()
    q = q_ref[...] * jnp.asarray(HEAD_DIM ** -0.5, _bf16)
    uu = uu_ref[...]
    strict = (lax.broadcasted_iota(jnp.int32, (tk, tk), 1)
              < lax.broadcasted_iota(jnp.int32, (tk, tk), 0))
    sign_bit = jnp.uint32(0x80000000)

    def mask_top(a, diagonal):
        if not diagonal:
            return a
        top = jnp.where(strict, a[:tk], 0.0)
        return top if a.shape[0] == tk else jnp.concatenate([top, a[tk:]], axis=0)

    def add_rows(x, r0, r1, delta):
        parts = ([x[:r0]] if r0 else []) + [x[r0:r1] + delta] + ([x[r1:]] if r1 < tq else [])
        return parts[0] if len(parts) == 1 else jnp.concatenate(parts, axis=0)

    def score_blocks(blocks):
        values, logits, afters = [], {}, {}
        for b, (jb, r0, r1, _) in enumerate(blocks):
            start = pl.multiple_of(jb * tk, tk)
            z = lax.dot_general(q[r0:r1], _by_head(k_ref[pl.ds(start, tk), :], head_lanes),
                                (((1,), (1,)), ((), ())), preferred_element_type=_f32)
            logits[b, 0], logits[b, 1] = z[:, :tk] * LOG2_E, z[:, tk:] * LOG2_E
            values.append(_by_head(v_ref[pl.ds(start, tk), :], head_lanes))
        for b, (_, _, _, diagonal) in enumerate(blocks):
            for h in range(2):
                z = logits[b, h]
                neg_abs = lax.bitcast_convert_type(lax.bitcast_convert_type(z, jnp.uint32) | sign_bit, _f32)
                softplus = jnp.maximum(z, 0.0) + jnp.log2(1.0 + jnp.exp2(neg_abs))
                sp = mask_top(softplus, diagonal)
                hi = sp.astype(_bf16)
                lo = (sp - hi.astype(_f32)).astype(_bf16)
                afters[b, h] = jnp.dot(jnp.concatenate([hi, lo], axis=1), uu,
                                       preferred_element_type=_f32)
        return values, logits, afters

    def weigh_blocks(blocks, scores, state):
        values, logits, afters = scores
        carries, acc = list(state[:2]), state[2]
        for b, (_, r0, r1, diagonal) in enumerate(blocks):
            weights = []
            for h in range(2):
                after = afters[b, h]
                w = mask_top(jnp.exp2(logits[b, h] + after + carries[h][r0:r1]), diagonal)
                weights.append(w.astype(_bf16))
                carries[h] = add_rows(carries[h], r0, r1, jnp.broadcast_to(after[:, :1], after.shape))
            pv = jnp.dot(jnp.concatenate(weights, axis=1), values[b], preferred_element_type=_f32)
            acc = add_rows(acc, r0, r1, pv)
        return carries[0], carries[1], acc

    def key_blocks(blocks, state):
        return weigh_blocks(blocks, score_blocks(blocks), state)

    blocks_per_tile = tq // tk
    first_diag = qi * blocks_per_tile

    skipped_from = [min(tq, max(NEAR_KEYS - d * tk, 0)) for d in range(blocks_per_tile)]
    partly_skipped = [d for d in range(blocks_per_tile) if skipped_from[d] < tq]

    near_blocks = [(first_diag + d, d * tk, tq, True) for d in reversed(range(blocks_per_tile))]
    if not first_tile:
        near_blocks += [(first_diag - 1 - d, 0, skipped_from[d], False)
                        for d in range(blocks_per_tile) if skipped_from[d]]
    def carry_max(state):
        return jnp.max(jnp.maximum(state[0], state[1]))

    def save(state):
        carry_ref[slot, 0], carry_ref[slot, 1], acc_ref[slot] = state
        return state

    def load():
        return carry_ref[slot, 0], carry_ref[slot, 1], acc_ref[slot]

    def more(loop):
        t, c_max = loop
        return (t < qi) & (c_max > EXP2_UNDERFLOW)

    def older_keys(loop):
        t, _ = loop
        blocks = [(first_diag - 1 - t * blocks_per_tile - d, 0, tq, False) for d in range(blocks_per_tile)]
        return t + 1, carry_max(save(key_blocks(blocks, load())))

    if older:
        for d in partly_skipped:
            save(key_blocks([(first_diag - 1 - d, skipped_from[d], tq, False)], load()))
        lax.while_loop(more, older_keys, (jnp.int32(1), c_max_ref[slot]))
        o_ref[...] = acc_ref[slot].astype(o_ref.dtype)
        return None

    near_scores = score_blocks(near_blocks)

    def finish():
        zero = jnp.zeros((tq, LANES), _f32)
        state = weigh_blocks(near_blocks, near_scores, (zero, zero, zero))
        o_ref[...] = state[2].astype(o_ref.dtype)
        if not first_tile:
            c_max_ref[slot] = carry_max(save(state))

    return finish


def _cumsum_matrix(tk):
    src = jnp.arange(2 * tk)[:, None] % tk
    dst = jnp.arange(tk)[None, :]
    return -(src >= dst).astype(_bf16)


def _stickbreak(qkv, *, batch, seq, d_model, tq=256, tk=128, per_trip=15):
    pairs = d_model // LANES
    return pl.pallas_call(
        functools.partial(_stickbreak_kernel, tq=tq, tk=tk, per_trip=per_trip),
        grid=(batch, pairs),
        in_specs=[
            pl.BlockSpec((None, seq, LANES), lambda b, p: (b, 0, p)),
            pl.BlockSpec((None, seq, LANES), lambda b, p: (b, 0, pairs + p)),
            pl.BlockSpec((None, seq, LANES), lambda b, p: (b, 0, 2 * pairs + p)),
            pl.BlockSpec((2 * tk, tk), lambda b, p: (0, 0)),
        ],
        out_specs=pl.BlockSpec((None, seq, LANES), lambda b, p: (b, 0, p)),
        out_shape=jax.ShapeDtypeStruct((batch, seq, d_model), _bf16),
        scratch_shapes=[pltpu.VMEM((per_trip, tq, LANES), _f32), pltpu.VMEM((per_trip, 2, tq, tk), _f32),
                        pltpu.SMEM((per_trip,), _f32)],
        compiler_params=_compiler_params(("parallel", "parallel")),
        name="stickbreak",
    )(qkv, qkv, qkv, _cumsum_matrix(tk))


def _dilated_kernel(q_ref, k_ref, v_ref, o_ref, m_ref, l_ref, acc_ref, bias_ref, *, group):
    per_class = q_ref.shape[1]
    tile_start = pl.program_id(2) * per_class
    head_lanes = _head_lanes()
    scale = jnp.asarray(HEAD_DIM ** -0.5, _bf16)
    row = lax.broadcasted_iota(jnp.int32, (BLOCK, 2 * BLOCK), 0)
    col = lax.broadcasted_iota(jnp.int32, (BLOCK, 2 * BLOCK), 1)
    in_prev = col < BLOCK
    ones_by_head = ((lax.broadcasted_iota(jnp.int32, (4 * BLOCK, LANES), 0) < 2 * BLOCK)
                    == (lax.broadcasted_iota(jnp.int32, (4 * BLOCK, LANES), 1) < HEAD_DIM)
                    ).astype(_f32).astype(_bf16)
    stages = []

    for branch, (window, r) in enumerate(DILATED_BRANCHES):
        assert window // r == BLOCK
        n_sub = N_CLASSES // r
        chunk = BLOCK // n_sub
        blocks_per_class = per_class // chunk
        sub_index = lambda a: n_sub * (a % chunk) + a // chunk
        q_sub, k_sub = sub_index(row), sub_index(col % BLOCK)
        band_prev = in_prev & (k_sub >= q_sub)
        band_cur = (~in_prev) & (k_sub <= q_sub)
        bias_ref[branch, 0] = jnp.where(band_cur, 0.0, MASK_VALUE)
        bias_ref[branch, 1] = jnp.where(band_prev | band_cur, 0.0, MASK_VALUE)

        def score_stage(g, r=r, branch=branch, n_sub=n_sub, chunk=chunk, blocks_per_class=blocks_per_class):
            places, vs, biases, logits = [], [], [], {}
            for u in range(group):
                idx = g * group + u
                c_r = idx // blocks_per_class
                local = (idx % blocks_per_class) * chunk
                cur = tile_start + local
                prev = jnp.maximum(cur - chunk, 0)
                classes = [c_r + r * m for m in range(n_sub)]

                def rows(ref, start):
                    start = start if isinstance(start, int) else pl.multiple_of(start, chunk)
                    return jnp.concatenate([ref[c, pl.ds(start, chunk), :] for c in classes], axis=0)

                q = rows(q_ref, local).astype(_bf16) * scale
                k = jnp.concatenate([rows(k_ref, prev), rows(k_ref, cur)], axis=0).astype(_bf16)
                v = jnp.concatenate([rows(v_ref, prev), rows(v_ref, cur)], axis=0).astype(_bf16)
                places.append((classes, local))
                vs.append(jnp.concatenate([_by_head(v, head_lanes), ones_by_head], axis=1))
                biases.append(bias_ref[branch, 1] if local >= chunk
                              else bias_ref[branch, (cur >= chunk).astype(jnp.int32)])
                for h in range(2):
                    q_head = jnp.where(head_lanes[h], q, jnp.zeros_like(q))
                    logits[u, h] = lax.dot_general(q_head, k, (((1,), (1,)), ((), ())),
                                                   preferred_element_type=_f32)
            return branch, chunk, places, vs, biases, logits

        stages += [functools.partial(score_stage, g) for g in range(r * blocks_per_class // group)]

    def finish_stage(branch, chunk, places, vs, biases, logits):
        probs, maxes = {}, {}
        for u in range(group):
            for h in range(2):
                z = logits[u, h] + biases[u]
                maxes[u, h] = jnp.max(z, axis=1, keepdims=True)
                probs[u, h] = jnp.exp(z - maxes[u, h]).astype(_bf16)
        for u in range(group):
            classes, local = places[u]
            pv_l = jnp.dot(jnp.concatenate([probs[u, 0], probs[u, 1]], axis=1), vs[u],
                           preferred_element_type=_f32)
            pv_t, l_t = pv_l[:, :LANES], pv_l[:, LANES:]
            m_t = jnp.where(head_lanes[0], maxes[u, 0], maxes[u, 1])
            for m, c in enumerate(classes):
                part = slice(m * chunk, (m + 1) * chunk)
                place = (c, pl.ds(local, chunk), slice(None))
                if branch == 0:
                    m_ref[place] = m_t[part]
                    l_ref[place] = l_t[part]
                    acc_ref[place] = pv_t[part]
                else:
                    m_old = m_ref[place]
                    m_new = jnp.maximum(m_old, m_t[part])
                    a_old = jnp.exp(m_old - m_new)
                    a_new = jnp.exp(m_t[part] - m_new)
                    m_ref[place] = m_new
                    l_ref[place] = a_old * l_ref[place] + a_new * l_t[part]
                    acc_ref[place] = a_old * acc_ref[place] + a_new * pv_t[part]

    scores = stages[0]()
    for nxt in stages[1:] + [None]:
        ahead = nxt() if nxt is not None else None
        finish_stage(*scores)
        scores = ahead

    o_ref[...] = (acc_ref[...] / l_ref[...]).astype(o_ref.dtype)


def _dilated(q, kv, *, batch, seq, d_model, group=4):
    pairs = d_model // LANES
    per_class = seq // N_CLASSES
    assert per_class % BLOCK == 0 and N_CLASSES % group == 0
    return pl.pallas_call(
        functools.partial(_dilated_kernel, group=group),
        grid=(batch, pairs, per_class // BLOCK),
        in_specs=[
            pl.BlockSpec((None, N_CLASSES, BLOCK, LANES), lambda b, p, i: (b, 0, i, p)),
            pl.BlockSpec((None, N_CLASSES, per_class, LANES), lambda b, p, i: (b, 0, 0, p)),
            pl.BlockSpec((None, N_CLASSES, per_class, LANES), lambda b, p, i: (b, 0, 0, pairs + p)),
        ],
        out_specs=pl.BlockSpec((None, N_CLASSES, BLOCK, LANES), lambda b, p, i: (b, 0, i, p)),
        out_shape=jax.ShapeDtypeStruct((batch, N_CLASSES, per_class, d_model), _bf16),
        scratch_shapes=[pltpu.VMEM((N_CLASSES, BLOCK, LANES), _f32)] * 3
        + [pltpu.VMEM((len(DILATED_BRANCHES), 2, BLOCK, 2 * BLOCK), _f32)],
        compiler_params=_compiler_params(("parallel", "parallel", "arbitrary")),
        name="dilated",
    )(q, kv, kv)


def _post_attn_kernel(*refs, tf, final_norm, reorder, next_query):
    h_ref, o_ref, wo_ref, g_ref, wg_ref, wu_ref, wd_ref, gf_ref, *refs = refs
    if next_query:
        gq_ref, wq_ref, cos_ref, sin_ref, out_ref, q_ref, acc_ref, *maybe_rows_ref = refs
    else:
        out_ref, acc_ref, *maybe_rows_ref = refs
    tm, d = acc_ref.shape
    rows_per_class = tm // N_CLASSES
    h_in, o_in = h_ref[...], o_ref[...]
    if reorder == "from_class_major":
        h_in, o_in = h_in.reshape(tm, d), o_in.reshape(tm, d)
    h1 = h_in + jnp.dot(o_in, wo_ref[...], preferred_element_type=_f32)
    xn = _rms(h1, g_ref[...]).astype(_bf16)
    acc_ref[...] = h1
    for c in range(wg_ref.shape[1] // tf):
        cols = slice(c * tf, (c + 1) * tf)
        gate = jnp.dot(xn, wg_ref[:, cols], preferred_element_type=_f32)
        up = jnp.dot(xn, wu_ref[:, cols], preferred_element_type=_f32)
        act = (gate / (1.0 + jnp.exp(-gate)) * up).astype(_bf16)
        acc_ref[...] += jnp.dot(act, wd_ref[cols, :], preferred_element_type=_f32)
    out = acc_ref[...]
    if final_norm:
        out = _rms(out, gf_ref[...])
    if reorder is None:
        out_ref[...] = out
        if next_query:
            _norm_project(out, cos_ref, sin_ref, [(gq_ref, wq_ref, q_ref, q_ref.shape[1])])
        return
    rows_ref, = maybe_rows_ref
    for j in range(d // LANES):
        lanes = slice(j * LANES, (j + 1) * LANES)
        for c in range(N_CLASSES):
            strided = pl.ds(c, rows_per_class, stride=N_CLASSES)
            if reorder == "to_class_major":
                if c == 0:
                    rows_ref[j] = out[:, lanes]
                out_ref[c, :, lanes] = rows_ref[j, strided, :]
            else:
                rows_ref[j, strided, :] = out[c * rows_per_class:(c + 1) * rows_per_class, lanes]
        if reorder == "from_class_major":
            out_ref[:, lanes] = rows_ref[j]


def _post_attn(h, o, wo, g, wg, wu, wd, g_final, *, final_norm, seq, reorder=None, next_query=None,
               tm=512, tf=256):
    m, d = h.shape
    f = wg.shape[1]
    assert seq % tm == 0 and tm % (8 * N_CLASSES) == 0 and f % tf == 0
    assert not (reorder and next_query)
    tiles = seq // tm
    per_class = seq // N_CLASSES
    resident = lambda shape: pl.BlockSpec(shape, lambda i: (0, 0), pipeline_mode=pl.Buffered(1))
    token_rows = pl.BlockSpec((tm, d), lambda i: (i, 0))
    class_rows = pl.BlockSpec((None, N_CLASSES, tm // N_CLASSES, d), lambda i: (i // tiles, 0, i % tiles, 0))
    class_major = lambda a: a.reshape(m // seq, N_CLASSES, per_class, d)
    rows_in = token_rows
    rows_out, out_shape = token_rows, jax.ShapeDtypeStruct((m, d), _f32)
    if reorder == "to_class_major":
        rows_out, out_shape = class_rows, jax.ShapeDtypeStruct((m // seq, N_CLASSES, per_class, d), _f32)
    elif reorder == "from_class_major":
        rows_in, h, o = class_rows, class_major(h), class_major(o)
    operands = [h, o, wo, g, wg, wu, wd, g_final]
    in_specs = [rows_in, rows_in, resident((d, d)), resident((1, d)), resident((d, f)), resident((d, f)),
                resident((f, d)), resident((1, d))]
    out_specs, out_shapes = [rows_out], [out_shape]
    if next_query:
        g_q, w_q, cos, sin = next_query
        n_q = w_q.shape[1]
        assert n_q % MXU_WIDTH == 0
        table_rows = pl.BlockSpec((tm, LANES), lambda i: (i % tiles, 0))
        operands += [g_q, w_q, cos, sin]
        in_specs += [resident((1, d)), resident((d, n_q)), table_rows, table_rows]
        out_specs.append(pl.BlockSpec((tm, n_q), lambda i: (i, 0)))
        out_shapes.append(jax.ShapeDtypeStruct((m, n_q), _f32))
    out, *q = pl.pallas_call(
        functools.partial(_post_attn_kernel, tf=tf, final_norm=final_norm, reorder=reorder,
                          next_query=bool(next_query)),
        grid=(m // tm,),
        in_specs=in_specs,
        out_specs=out_specs,
        out_shape=out_shapes,
        scratch_shapes=[pltpu.VMEM((tm, d), _f32)]
        + ([pltpu.VMEM((d // LANES, tm, LANES), _f32)] if reorder else []),
        compiler_params=_compiler_params(("parallel",)),
        name="post_attn",
    )(*operands)
    return (out.reshape(m, d), *q) if next_query else out.reshape(m, d)


def _rope_tables(seq):
    inv_freq = ROPE_THETA ** (-jnp.arange(0, HEAD_DIM, 2, dtype=_f32) / HEAD_DIM)
    ang = jnp.arange(seq, dtype=_f32)[:, None] * inv_freq[None, :]
    cos, sin = jnp.cos(ang), jnp.sin(ang)
    reps = LANES // HEAD_DIM
    return (jnp.tile(jnp.concatenate([cos, cos], axis=1), (1, reps)),
            jnp.tile(jnp.concatenate([-sin, sin], axis=1), (1, reps)))


def _class_major(a, lead):
    s, f = a.shape[-2:]
    return jnp.swapaxes(a.reshape(*lead, s // N_CLASSES, N_CLASSES, f), -3, -2)


def kernel(x, norm_mix, w_qkv_a, w_o_a, norm_kv, w_kv, w_q_b, w_o_b, norm_ffn, w_gate, w_up, w_down,
           norm_final):
    b, s, d = x.shape
    depth = norm_mix.shape[0]
    n_a = w_qkv_a.shape[0]
    assert d % LANES == 0 and s % (N_CLASSES * BLOCK) == 0
    cos, sin = _rope_tables(s)
    cos_cm, sin_cm = (_class_major(t, ()).reshape(s, LANES) for t in (cos, sin))
    cast = lambda w: w.astype(_bf16)
    w_qkv_a, w_o_a, w_kv, w_q_b, w_o_b = map(cast, (w_qkv_a, w_o_a, w_kv, w_q_b, w_o_b))
    w_gate, w_up, w_down = map(cast, (w_gate, w_up, w_down))
    g_final = norm_final.reshape(1, d)
    per_class = s // N_CLASSES

    reorder = {}
    if 0 < n_a < depth - 1:
        reorder = {n_a - 1: "to_class_major", depth - 1: "from_class_major"}

    h = x.reshape(b * s, d)
    kv = q = None
    for layer in range(depth):
        g_mix = norm_mix[layer].reshape(1, d)
        if layer < n_a:
            qkv, = _norm_proj(h, [(g_mix, w_qkv_a[layer], 0, _bf16)], cos, sin, seq=s)
            o = _stickbreak(qkv.reshape(b, s, 3 * d), batch=b, seq=s, d_model=d)
            w_o = w_o_a[layer]
        else:
            j = layer - n_a
            if kv is None:
                if not reorder:
                    h = _class_major(h.reshape(b, s, d), (b,)).reshape(b * s, d)
                q, kv = _norm_proj(h, [(g_mix, w_q_b[j], d, _f32), (norm_kv.reshape(1, d), w_kv, d, _f32)],
                                   cos_cm, sin_cm, seq=s)
                kv = kv.reshape(b, N_CLASSES, per_class, 2 * d)
            elif q is None:
                q, = _norm_proj(h, [(g_mix, w_q_b[j], d, _f32)], cos_cm, sin_cm, seq=s)
            o = _dilated(q.reshape(b, N_CLASSES, per_class, d), kv, batch=b, seq=s, d_model=d)
            w_o = w_o_b[j]
        nxt = layer + 1
        fuse_query = kv is not None and nxt < depth and not reorder.get(layer)
        next_query = ((norm_mix[nxt].reshape(1, d), w_q_b[nxt - n_a], cos_cm, sin_cm) if fuse_query else None)
        h = _post_attn(h, o.reshape(b * s, d), w_o, norm_ffn[layer].reshape(1, d), w_gate[layer],
                       w_up[layer], w_down[layer], g_final, final_norm=(layer == depth - 1), seq=s,
                       reorder=reorder.get(layer), next_query=next_query)
        h, q = h if fuse_query else (h, None)
    if kv is not None and not reorder:
        h = jnp.swapaxes(h.reshape(b, N_CLASSES, per_class, d), 1, 2)
    return h.reshape(b, s, d)
```

```python
import functools

import jax
import jax.numpy as jnp
from jax import lax
from jax.experimental import pallas as pl
from jax.experimental.pallas import tpu as pltpu

HEAD_DIM = 64
BLOCK = 128
ROPE_THETA = 10000.0
NORM_EPS = 1e-6
DILATED_BRANCHES = ((128, 1), (512, 4), (2048, 16))
N_CLASSES = max(r for _, r in DILATED_BRANCHES)

LANES = 128
VMEM_LIMIT_BYTES = 56 * 1024 * 1024
MASK_VALUE = -1e30
EXP2_UNDERFLOW = -160.0
LOG2_E = 1.4426950408889634
NEAR_KEYS = 176
MXU_WIDTH = 256
_f32 = jnp.float32
_bf16 = jnp.bfloat16


def _rms(x, g):
    return x * lax.rsqrt(jnp.mean(x * x, axis=-1, keepdims=True) + NORM_EPS) * g


def _compiler_params(semantics):
    return pltpu.CompilerParams(dimension_semantics=semantics, vmem_limit_bytes=VMEM_LIMIT_BYTES)


def _head_lanes():
    lane = lax.broadcasted_iota(jnp.int32, (1, LANES), 1)
    return lane < HEAD_DIM, lane >= HEAD_DIM


def _by_head(x, head_lanes):
    return jnp.concatenate([jnp.where(hm, x, jnp.zeros_like(x)) for hm in head_lanes], axis=0)


def _norm_project(x, cos_ref, sin_ref, projections):
    x_hat = x * lax.rsqrt(jnp.mean(x * x, axis=-1, keepdims=True) + NORM_EPS)
    tn = MXU_WIDTH
    if any(p[3] for p in projections):
        reps = tn // LANES
        cos = jnp.concatenate([cos_ref[...]] * reps, axis=1)
        sin = jnp.concatenate([sin_ref[...]] * reps, axis=1)
        first_half = (lax.broadcasted_iota(jnp.int32, cos.shape, 1) % HEAD_DIM) < HEAD_DIM // 2
    for g_ref, w_ref, o_ref, rope_cols in projections:
        xn = (x_hat * g_ref[...]).astype(_bf16)
        for c in range(o_ref.shape[1] // tn):
            cols = slice(c * tn, (c + 1) * tn)
            y = jnp.dot(xn, w_ref[:, cols], preferred_element_type=_f32)
            if c * tn < rope_cols:
                partner = jnp.where(first_half, pltpu.roll(y, tn - HEAD_DIM // 2, 1),
                                    pltpu.roll(y, HEAD_DIM // 2, 1))
                y = y * cos + partner * sin
            o_ref[:, cols] = y.astype(o_ref.dtype)


def _norm_proj_kernel(x_ref, cos_ref, sin_ref, *refs, rope_cols):
    n_proj = len(rope_cols)
    _norm_project(x_ref[...], cos_ref, sin_ref,
                  [(refs[2 * p], refs[2 * p + 1], refs[2 * n_proj + p], rope_cols[p]) for p in range(n_proj)])


def _norm_proj(x, projections, cos, sin, *, seq, tm=512):
    m, d = x.shape
    assert m % tm == 0 and seq % tm == 0
    pos_blocks = seq // tm
    operands, in_specs, out_specs, out_shapes = [], [], [], []
    for g, w, rope_cols, out_dtype in projections:
        n_out = w.shape[1]
        assert n_out % MXU_WIDTH == 0 and rope_cols % MXU_WIDTH == 0
        operands += [g, w]
        in_specs += [pl.BlockSpec((1, d), lambda i: (0, 0)), pl.BlockSpec((d, n_out), lambda i: (0, 0))]
        out_specs.append(pl.BlockSpec((tm, n_out), lambda i: (i, 0)))
        out_shapes.append(jax.ShapeDtypeStruct((m, n_out), out_dtype))
    return pl.pallas_call(
        functools.partial(_norm_proj_kernel, rope_cols=tuple(p[2] for p in projections)),
        grid=(m // tm,),
        in_specs=[
            pl.BlockSpec((tm, d), lambda i: (i, 0)),
            pl.BlockSpec((tm, LANES), lambda i: (i % pos_blocks, 0)),
            pl.BlockSpec((tm, LANES), lambda i: (i % pos_blocks, 0)),
        ] + in_specs,
        out_specs=out_specs,
        out_shape=out_shapes,
        compiler_params=_compiler_params(("parallel",)),
        name="norm_proj",
    )(x, cos, sin, *operands)


def _stickbreak_kernel(q_ref, k_ref, v_ref, uu_ref, o_ref, acc_ref, carry_ref, c_max_ref,
                       *, tq, tk, per_trip):
    def start_tile(qi, slot=0, first_tile=False, older=False):
        rows = pl.ds(qi * tq if first_tile else pl.multiple_of(qi * tq, tq), tq)
        return _stickbreak_tile(qi, q_ref.at[rows], k_ref, v_ref, uu_ref, o_ref.at[rows], acc_ref, carry_ref,
                                c_max_ref, tq=tq, tk=tk, slot=slot, first_tile=first_tile, older=older)

    def tiles(trip, carry):
        first = 1 + trip * per_trip
        finish = start_tile(first, 0)
        for j in range(per_trip):
            ahead = start_tile(first + j + 1, j + 1) if j + 1 < per_trip else None
            finish()
            finish = ahead

        def older_keys(j, c):
            @pl.when(c_max_ref[j] > EXP2_UNDERFLOW)
            def _():
                start_tile(first + j, j, older=True)
            return c

        lax.fori_loop(0, per_trip, older_keys, 0)
        return carry

    n_tiles = q_ref.shape[0] // tq
    assert (n_tiles - 1) % per_trip == 0
    start_tile(0, first_tile=True)()
    lax.fori_loop(0, (n_tiles - 1) // per_trip, tiles, 0)


def _stickbreak_tile(qi, q_ref, k_ref, v_ref, uu_ref, o_ref, acc_ref, carry_ref, c_max_ref,
                     *, tq, tk, slot, first_tile, older):
    head_lanes = _head_lanes()
    q = q_ref[...] * jnp.asarray(HEAD_DIM ** -0.5, _bf16)
    uu = uu_ref[...]
    strict = (lax.broadcasted_iota(jnp.int32, (tk, tk), 1)
              < lax.broadcasted_iota(jnp.int32, (tk, tk), 0))
    sign_bit = jnp.uint32(0x80000000)

    def mask_top(a, diagonal):
        if not diagonal:
            return a
        top = jnp.where(strict, a[:tk], 0.0)
        return top if a.shape[0] == tk else jnp.concatenate([top, a[tk:]], axis=0)

    def add_rows(x, r0, r1, delta):
        parts = ([x[:r0]] if r0 else []) + [x[r0:r1] + delta] + ([x[r1:]] if r1 < tq else [])
        return parts[0] if len(parts) == 1 else jnp.concatenate(parts, axis=0)

    def score_blocks(blocks):
        values, logits, afters = [], {}, {}
        for b, (jb, r0, r1, _) in enumerate(blocks):
            start = pl.multiple_of(jb * tk, tk)
            z = lax.dot_general(q[r0:r1], _by_head(k_ref[pl.ds(start, tk), :], head_lanes),
                                (((1,), (1,)), ((), ())), preferred_element_type=_f32)
            logits[b, 0], logits[b, 1] = z[:, :tk] * LOG2_E, z[:, tk:] * LOG2_E
            values.append(_by_head(v_ref[pl.ds(start, tk), :], head_lanes))
        for b, (_, _, _, diagonal) in enumerate(blocks):
            for h in range(2):
                z = logits[b, h]
                neg_abs = lax.bitcast_convert_type(lax.bitcast_convert_type(z, jnp.uint32) | sign_bit, _f32)
                softplus = jnp.maximum(z, 0.0) + jnp.log2(1.0 + jnp.exp2(neg_abs))
                sp = mask_top(softplus, diagonal)
                hi = sp.astype(_bf16)
                lo = (sp - hi.astype(_f32)).astype(_bf16)
                afters[b, h] = jnp.dot(jnp.concatenate([hi, lo], axis=1), uu,
                                       preferred_element_type=_f32)
        return values, logits, afters

    def weigh_blocks(blocks, scores, state):
        values, logits, afters = scores
        carries, acc = list(state[:2]), state[2]
        for b, (_, r0, r1, diagonal) in enumerate(blocks):
            weights = []
            for h in range(2):
                after = afters[b, h]
                w = mask_top(jnp.exp2(logits[b, h] + after + carries[h][r0:r1]), diagonal)
                weights.append(w.astype(_bf16))
                carries[h] = add_rows(carries[h], r0, r1, jnp.broadcast_to(after[:, :1], after.shape))
            pv = jnp.dot(jnp.concatenate(weights, axis=1), values[b], preferred_element_type=_f32)
            acc = add_rows(acc, r0, r1, pv)
        return carries[0], carries[1], acc

    def key_blocks(blocks, state):
        return weigh_blocks(blocks, score_blocks(blocks), state)

    blocks_per_tile = tq // tk
    first_diag = qi * blocks_per_tile

    skipped_from = [min(tq, max(NEAR_KEYS - d * tk, 0)) for d in range(blocks_per_tile)]
    partly_skipped = [d for d in range(blocks_per_tile) if skipped_from[d] < tq]

    near_blocks = [(first_diag + d, d * tk, tq, True) for d in reversed(range(blocks_per_tile))]
    if not first_tile:
        near_blocks += [(first_diag - 1 - d, 0, skipped_from[d], False)
                        for d in range(blocks_per_tile) if skipped_from[d]]
    def carry_max(state):
        return jnp.max(jnp.maximum(state[0], state[1]))

    def save(state):
        carry_ref[slot, 0], carry_ref[slot, 1], acc_ref[slot] = state
        return state

    def load():
        return carry_ref[slot, 0], carry_ref[slot, 1], acc_ref[slot]

    def more(loop):
        t, c_max = loop
        return (t < qi) & (c_max > EXP2_UNDERFLOW)

    def older_keys(loop):
        t, _ = loop
        blocks = [(first_diag - 1 - t * blocks_per_tile - d, 0, tq, False) for d in range(blocks_per_tile)]
        return t + 1, carry_max(save(key_blocks(blocks, load())))

    if older:
        for d in partly_skipped:
            save(key_blocks([(first_diag - 1 - d, skipped_from[d], tq, False)], load()))
        lax.while_loop(more, older_keys, (jnp.int32(1), c_max_ref[slot]))
        o_ref[...] = acc_ref[slot].astype(o_ref.dtype)
        return None

    near_scores = score_blocks(near_blocks)

    def finish():
        zero = jnp.zeros((tq, LANES), _f32)
        state = weigh_blocks(near_blocks, near_scores, (zero, zero, zero))
        o_ref[...] = state[2].astype(o_ref.dtype)
        if not first_tile:
            c_max_ref[slot] = carry_max(save(state))

    return finish


def _cumsum_matrix(tk):
    src = jnp.arange(2 * tk)[:, None] % tk
    dst = jnp.arange(tk)[None, :]
    return -(src >= dst).astype(_bf16)


def _stickbreak(qkv, *, batch, seq, d_model, tq=256, tk=128, per_trip=15):
    pairs = d_model // LANES
    return pl.pallas_call(
        functools.partial(_stickbreak_kernel, tq=tq, tk=tk, per_trip=per_trip),
        grid=(batch, pairs),
        in_specs=[
            pl.BlockSpec((None, seq, LANES), lambda b, p: (b, 0, p)),
            pl.BlockSpec((None, seq, LANES), lambda b, p: (b, 0, pairs + p)),
            pl.BlockSpec((None, seq, LANES), lambda b, p: (b, 0, 2 * pairs + p)),
            pl.BlockSpec((2 * tk, tk), lambda b, p: (0, 0)),
        ],
        out_specs=pl.BlockSpec((None, seq, LANES), lambda b, p: (b, 0, p)),
        out_shape=jax.ShapeDtypeStruct((batch, seq, d_model), _bf16),
        scratch_shapes=[pltpu.VMEM((per_trip, tq, LANES), _f32), pltpu.VMEM((per_trip, 2, tq, tk), _f32),
                        pltpu.SMEM((per_trip,), _f32)],
        compiler_params=_compiler_params(("parallel", "parallel")),
        name="stickbreak",
    )(qkv, qkv, qkv, _cumsum_matrix(tk))


def _dilated_kernel(q_ref, k_ref, v_ref, o_ref, m_ref, l_ref, acc_ref, bias_ref, *, group):
    per_class = q_ref.shape[1]
    tile_start = pl.program_id(2) * per_class
    head_lanes = _head_lanes()
    scale = jnp.asarray(HEAD_DIM ** -0.5, _bf16)
    row = lax.broadcasted_iota(jnp.int32, (BLOCK, 2 * BLOCK), 0)
    col = lax.broadcasted_iota(jnp.int32, (BLOCK, 2 * BLOCK), 1)
    in_prev = col < BLOCK
    ones_by_head = ((lax.broadcasted_iota(jnp.int32, (4 * BLOCK, LANES), 0) < 2 * BLOCK)
                    == (lax.broadcasted_iota(jnp.int32, (4 * BLOCK, LANES), 1) < HEAD_DIM)
                    ).astype(_f32).astype(_bf16)
    stages = []

    for branch, (window, r) in enumerate(DILATED_BRANCHES):
        assert window // r == BLOCK
        n_sub = N_CLASSES // r
        chunk = BLOCK // n_sub
        blocks_per_class = per_class // chunk
        sub_index = lambda a: n_sub * (a % chunk) + a // chunk
        q_sub, k_sub = sub_index(row), sub_index(col % BLOCK)
        band_prev = in_prev & (k_sub >= q_sub)
        band_cur = (~in_prev) & (k_sub <= q_sub)
        bias_ref[branch, 0] = jnp.where(band_cur, 0.0, MASK_VALUE)
        bias_ref[branch, 1] = jnp.where(band_prev | band_cur, 0.0, MASK_VALUE)

        def score_stage(g, r=r, branch=branch, n_sub=n_sub, chunk=chunk, blocks_per_class=blocks_per_class):
            places, vs, biases, logits = [], [], [], {}
            for u in range(group):
                idx = g * group + u
                c_r = idx // blocks_per_class
                local = (idx % blocks_per_class) * chunk
                cur = tile_start + local
                prev = jnp.maximum(cur - chunk, 0)
                classes = [c_r + r * m for m in range(n_sub)]

                def rows(ref, start):
                    start = start if isinstance(start, int) else pl.multiple_of(start, chunk)
                    return jnp.concatenate([ref[c, pl.ds(start, chunk), :] for c in classes], axis=0)

                q = rows(q_ref, local).astype(_bf16) * scale
                k = jnp.concatenate([rows(k_ref, prev), rows(k_ref, cur)], axis=0).astype(_bf16)
                v = jnp.concatenate([rows(v_ref, prev), rows(v_ref, cur)], axis=0).astype(_bf16)
                places.append((classes, local))
                vs.append(jnp.concatenate([_by_head(v, head_lanes), ones_by_head], axis=1))
                biases.append(bias_ref[branch, 1] if local >= chunk
                              else bias_ref[branch, (cur >= chunk).astype(jnp.int32)])
                for h in range(2):
                    q_head = jnp.where(head_lanes[h], q, jnp.zeros_like(q))
                    logits[u, h] = lax.dot_general(q_head, k, (((1,), (1,)), ((), ())),
                                                   preferred_element_type=_f32)
            return branch, chunk, places, vs, biases, logits

        stages += [functools.partial(score_stage, g) for g in range(r * blocks_per_class // group)]

    def finish_stage(branch, chunk, places, vs, biases, logits):
        probs, maxes = {}, {}
        for u in range(group):
            for h in range(2):
                z = logits[u, h] + biases[u]
                maxes[u, h] = jnp.max(z, axis=1, keepdims=True)
                probs[u, h] = jnp.exp(z - maxes[u, h]).astype(_bf16)
        for u in range(group):
            classes, local = places[u]
            pv_l = jnp.dot(jnp.concatenate([probs[u, 0], probs[u, 1]], axis=1), vs[u],
                           preferred_element_type=_f32)
            pv_t, l_t = pv_l[:, :LANES], pv_l[:, LANES:]
            m_t = jnp.where(head_lanes[0], maxes[u, 0], maxes[u, 1])
            for m, c in enumerate(classes):
                part = slice(m * chunk, (m + 1) * chunk)
                place = (c, pl.ds(local, chunk), slice(None))
                if branch == 0:
                    m_ref[place] = m_t[part]
                    l_ref[place] = l_t[part]
                    acc_ref[place] = pv_t[part]
                else:
                    m_old = m_ref[place]
                    m_new = jnp.maximum(m_old, m_t[part])
                    a_old = jnp.exp(m_old - m_new)
                    a_new = jnp.exp(m_t[part] - m_new)
                    m_ref[place] = m_new
                    l_ref[place] = a_old * l_ref[place] + a_new * l_t[part]
                    acc_ref[place] = a_old * acc_ref[place] + a_new * pv_t[part]

    scores = stages[0]()
    for nxt in stages[1:] + [None]:
        ahead = nxt() if nxt is not None else None
        finish_stage(*scores)
        scores = ahead

    o_ref[...] = (acc_ref[...] / l_ref[...]).astype(o_ref.dtype)


def _dilated(q, kv, *, batch, seq, d_model, group=4):
    pairs = d_model // LANES
    per_class = seq // N_CLASSES
    assert per_class % BLOCK == 0 and N_CLASSES % group == 0
    return pl.pallas_call(
        functools.partial(_dilated_kernel, group=group),
        grid=(batch, pairs, per_class // BLOCK),
        in_specs=[
            pl.BlockSpec((None, N_CLASSES, BLOCK, LANES), lambda b, p, i: (b, 0, i, p)),
            pl.BlockSpec((None, N_CLASSES, per_class, LANES), lambda b, p, i: (b, 0, 0, p)),
            pl.BlockSpec((None, N_CLASSES, per_class, LANES), lambda b, p, i: (b, 0, 0, pairs + p)),
        ],
        out_specs=pl.BlockSpec((None, N_CLASSES, BLOCK, LANES), lambda b, p, i: (b, 0, i, p)),
        out_shape=jax.ShapeDtypeStruct((batch, N_CLASSES, per_class, d_model), _bf16),
        scratch_shapes=[pltpu.VMEM((N_CLASSES, BLOCK, LANES), _f32)] * 3
        + [pltpu.VMEM((len(DILATED_BRANCHES), 2, BLOCK, 2 * BLOCK), _f32)],
        compiler_params=_compiler_params(("parallel", "parallel", "arbitrary")),
        name="dilated",
    )(q, kv, kv)


def _post_attn_kernel(*refs, tf, final_norm, reorder, next_query, next_rope_cols):
    h_ref, o_ref, wo_ref, g_ref, wg_ref, wu_ref, wd_ref, gf_ref, *refs = refs
    if next_query:
        gq_ref, wq_ref, cos_ref, sin_ref, out_ref, q_ref, acc_ref, *maybe_rows_ref = refs
    else:
        out_ref, acc_ref, *maybe_rows_ref = refs
    tm, d = acc_ref.shape
    rows_per_class = tm // N_CLASSES
    h_in, o_in = h_ref[...], o_ref[...]
    if reorder == "from_class_major":
        h_in, o_in = h_in.reshape(tm, d), o_in.reshape(tm, d)
    h1 = h_in + jnp.dot(o_in, wo_ref[...], preferred_element_type=_f32)
    xn = _rms(h1, g_ref[...]).astype(_bf16)
    acc_ref[...] = h1
    for c in range(wg_ref.shape[1] // tf):
        cols = slice(c * tf, (c + 1) * tf)
        gate = jnp.dot(xn, wg_ref[:, cols], preferred_element_type=_f32)
        up = jnp.dot(xn, wu_ref[:, cols], preferred_element_type=_f32)
        act = (gate / (1.0 + jnp.exp(-gate)) * up).astype(_bf16)
        acc_ref[...] += jnp.dot(act, wd_ref[cols, :], preferred_element_type=_f32)
    out = acc_ref[...]
    if final_norm:
        out = _rms(out, gf_ref[...])
    if reorder is None:
        out_ref[...] = out
        if next_query:
            _norm_project(out, cos_ref, sin_ref, [(gq_ref, wq_ref, q_ref, next_rope_cols)])
        return
    rows_ref, = maybe_rows_ref
    for j in range(d // LANES):
        lanes = slice(j * LANES, (j + 1) * LANES)
        for c in range(N_CLASSES):
            strided = pl.ds(c, rows_per_class, stride=N_CLASSES)
            if reorder == "to_class_major":
                if c == 0:
                    rows_ref[j] = out[:, lanes]
                out_ref[c, :, lanes] = rows_ref[j, strided, :]
            else:
                rows_ref[j, strided, :] = out[c * rows_per_class:(c + 1) * rows_per_class, lanes]
        if reorder == "from_class_major":
            out_ref[:, lanes] = rows_ref[j]


def _post_attn(h, o, wo, g, wg, wu, wd, g_final, *, final_norm, seq, reorder=None, next_query=None,
               tm=512, tf=256):
    m, d = h.shape
    f = wg.shape[1]
    assert seq % tm == 0 and tm % (8 * N_CLASSES) == 0 and f % tf == 0
    assert not (reorder and next_query)
    tiles = seq // tm
    per_class = seq // N_CLASSES
    resident = lambda shape: pl.BlockSpec(shape, lambda i: (0, 0), pipeline_mode=pl.Buffered(1))
    token_rows = pl.BlockSpec((tm, d), lambda i: (i, 0))
    class_rows = pl.BlockSpec((None, N_CLASSES, tm // N_CLASSES, d), lambda i: (i // tiles, 0, i % tiles, 0))
    class_major = lambda a: a.reshape(m // seq, N_CLASSES, per_class, d)
    rows_in = token_rows
    rows_out, out_shape = token_rows, jax.ShapeDtypeStruct((m, d), _f32)
    if reorder == "to_class_major":
        rows_out, out_shape = class_rows, jax.ShapeDtypeStruct((m // seq, N_CLASSES, per_class, d), _f32)
    elif reorder == "from_class_major":
        rows_in, h, o = class_rows, class_major(h), class_major(o)
    operands = [h, o, wo, g, wg, wu, wd, g_final]
    in_specs = [rows_in, rows_in, resident((d, d)), resident((1, d)), resident((d, f)), resident((d, f)),
                resident((f, d)), resident((1, d))]
    out_specs, out_shapes = [rows_out], [out_shape]
    next_rope_cols = 0
    if next_query:
        g_q, w_q, cos, sin, next_rope_cols, q_dtype = next_query
        n_q = w_q.shape[1]
        assert n_q % MXU_WIDTH == 0 and next_rope_cols % MXU_WIDTH == 0
        table_rows = pl.BlockSpec((tm, LANES), lambda i: (i % tiles, 0))
        operands += [g_q, w_q, cos, sin]
        in_specs += [resident((1, d)), resident((d, n_q)), table_rows, table_rows]
        out_specs.append(pl.BlockSpec((tm, n_q), lambda i: (i, 0)))
        out_shapes.append(jax.ShapeDtypeStruct((m, n_q), q_dtype))
    out, *q = pl.pallas_call(
        functools.partial(_post_attn_kernel, tf=tf, final_norm=final_norm, reorder=reorder,
                          next_query=bool(next_query), next_rope_cols=next_rope_cols),
        grid=(m // tm,),
        in_specs=in_specs,
        out_specs=out_specs,
        out_shape=out_shapes,
        scratch_shapes=[pltpu.VMEM((tm, d), _f32)]
        + ([pltpu.VMEM((d // LANES, tm, LANES), _f32)] if reorder else []),
        compiler_params=_compiler_params(("parallel",)),
        name="post_attn",
    )(*operands)
    return (out.reshape(m, d), *q) if next_query else out.reshape(m, d)


def _rope_tables(seq):
    inv_freq = ROPE_THETA ** (-jnp.arange(0, HEAD_DIM, 2, dtype=_f32) / HEAD_DIM)
    ang = jnp.arange(seq, dtype=_f32)[:, None] * inv_freq[None, :]
    cos, sin = jnp.cos(ang), jnp.sin(ang)
    reps = LANES // HEAD_DIM
    return (jnp.tile(jnp.concatenate([cos, cos], axis=1), (1, reps)),
            jnp.tile(jnp.concatenate([-sin, sin], axis=1), (1, reps)))


def _class_major(a, lead):
    s, f = a.shape[-2:]
    return jnp.swapaxes(a.reshape(*lead, s // N_CLASSES, N_CLASSES, f), -3, -2)


def kernel(x, norm_mix, w_qkv_a, w_o_a, norm_kv, w_kv, w_q_b, w_o_b, norm_ffn, w_gate, w_up, w_down,
           norm_final):
    b, s, d = x.shape
    depth = norm_mix.shape[0]
    n_a = w_qkv_a.shape[0]
    assert d % LANES == 0 and s % (N_CLASSES * BLOCK) == 0
    cos, sin = _rope_tables(s)
    cos_cm, sin_cm = (_class_major(t, ()).reshape(s, LANES) for t in (cos, sin))
    cast = lambda w: w.astype(_bf16)
    w_qkv_a, w_o_a, w_kv, w_q_b, w_o_b = map(cast, (w_qkv_a, w_o_a, w_kv, w_q_b, w_o_b))
    w_gate, w_up, w_down = map(cast, (w_gate, w_up, w_down))
    g_final = norm_final.reshape(1, d)
    per_class = s // N_CLASSES

    reorder = {}
    if 0 < n_a < depth - 1:
        reorder = {n_a - 1: "to_class_major", depth - 1: "from_class_major"}

    h = x.reshape(b * s, d)
    kv = q = None
    for layer in range(depth):
        g_mix = norm_mix[layer].reshape(1, d)
        if layer < n_a:
            qkv = q
            if qkv is None:
                qkv, = _norm_proj(h, [(g_mix, w_qkv_a[layer], 0, _bf16)], cos, sin, seq=s)
            o = _stickbreak(qkv.reshape(b, s, 3 * d), batch=b, seq=s, d_model=d)
            w_o = w_o_a[layer]
        else:
            j = layer - n_a
            if kv is None:
                if not reorder:
                    h = _class_major(h.reshape(b, s, d), (b,)).reshape(b * s, d)
                q, kv = _norm_proj(h, [(g_mix, w_q_b[j], d, _f32), (norm_kv.reshape(1, d), w_kv, d, _f32)],
                                   cos_cm, sin_cm, seq=s)
                kv = kv.reshape(b, N_CLASSES, per_class, 2 * d)
            elif q is None:
                q, = _norm_proj(h, [(g_mix, w_q_b[j], d, _f32)], cos_cm, sin_cm, seq=s)
            o = _dilated(q.reshape(b, N_CLASSES, per_class, d), kv, batch=b, seq=s, d_model=d)
            w_o = w_o_b[j]
        nxt = layer + 1
        next_query = None
        if nxt < n_a:
            next_query = (norm_mix[nxt].reshape(1, d), w_qkv_a[nxt], cos, sin, 0, _bf16)
        elif kv is not None and nxt < depth and not reorder.get(layer):
            next_query = (norm_mix[nxt].reshape(1, d), w_q_b[nxt - n_a], cos_cm, sin_cm, d, _f32)
        fuse_query = next_query is not None
        h = _post_attn(h, o.reshape(b * s, d), w_o, norm_ffn[layer].reshape(1, d), w_gate[layer],
                       w_up[layer], w_down[layer], g_final, final_norm=(layer == depth - 1), seq=s,
                       reorder=reorder.get(layer), next_query=next_query)
        h, q = h if fuse_query else (h, None)
    if kv is not None and not reorder:
        h = jnp.swapaxes(h.reshape(b, N_CLASSES, per_class, d), 1, 2)
    return h.reshape(b, s, d)
```
